```python
import jax, jax.numpy as jnp
from jax import lax
import numpy as np

D_MODEL = 1024
BATCH = 16
SEQ = 4096
DEPTH = 4

N_MIXERS = 2
EXPAND = 2
D_INNER = EXPAND * D_MODEL
RET_HEADS = 4
RET_QK_DIM = D_MODEL // RET_HEADS
RET_V_DIM = D_INNER // RET_HEADS
RET_CHUNK = 128
SB_HEADS = 16
SB_QK_DIM = D_MODEL // SB_HEADS
SB_V_DIM = D_INNER // SB_HEADS
SB_BLOCK = 128
D_PROJ = 2 * D_MODEL + 2 * D_INNER
N_RET_LAYERS = (DEPTH + 1) // 2
N_SB_LAYERS = DEPTH // 2
RMS_EPS = 1e-6
GN_EPS = 1e-5
ROPE_BASE = 10000.0

kernel_name = "hybrid_retention_stickbreaking_trunk"


def rms_norm(x, g):
    xf = x.astype(jnp.float32)
    y = xf * lax.rsqrt(jnp.mean(xf * xf, axis=-1, keepdims=True) + RMS_EPS)
    return (y * g.astype(jnp.float32)).astype(x.dtype)


def rotate(x, cos, sin):
    x1, x2 = jnp.split(x, 2, axis=-1)
    c = cos[None, :, None, :]
    s = sin[None, :, None, :]
    return jnp.concatenate([x1 * c - x2 * s, x2 * c + x1 * s], axis=-1)


def retention_branch(q, k, v, gn_g):
    B, S, _ = q.shape
    H, dk, dv, C = RET_HEADS, RET_QK_DIM, RET_V_DIM, RET_CHUNK
    n_chunks = S // C
    q = q.astype(jnp.float32).reshape(B, S, H, dk)
    k = k.astype(jnp.float32).reshape(B, S, H, dk)
    v = v.astype(jnp.float32).reshape(B, S, H, dv)
    omega = 1.0 / (ROPE_BASE ** jnp.linspace(0.0, 1.0, dk // 2, dtype=jnp.float32))
    ang = jnp.arange(S, dtype=jnp.float32)[:, None] * omega[None, :]
    cos, sin = jnp.cos(ang), jnp.sin(ang)
    q = rotate(q, cos, sin)
    k = rotate(k, cos, sin) * (dk ** -0.5)
    log_gamma = jnp.log1p(-jnp.exp2(jnp.linspace(-5.0, -9.0, H, dtype=jnp.float32)))
    idx = jnp.arange(C, dtype=jnp.float32)
    diff = idx[:, None] - idx[None, :]
    decay_mat = jnp.where(diff[None] >= 0,
                          jnp.exp(jnp.maximum(diff, 0.0)[None] * log_gamma[:, None, None]), 0.0)
    q_dec = jnp.exp((idx[None, :] + 1.0) * log_gamma[:, None])
    k_dec = jnp.exp((C - 1.0 - idx[None, :]) * log_gamma[:, None])
    chunk_dec = jnp.exp(C * log_gamma)

    def to_chunks(t, d):
        return t.reshape(B, n_chunks, C, H, d).transpose(1, 0, 3, 2, 4)

    qc, kc, vc = to_chunks(q, dk), to_chunks(k, dk), to_chunks(v, dv)

    def body(state, xs):
        qb, kb, vb = xs
        scores = jnp.einsum('bhid,bhjd->bhij', qb, kb) * decay_mat[None]
        inner = jnp.einsum('bhij,bhjv->bhiv', scores, vb)
        cross = jnp.einsum('bhid,bhdv->bhiv', qb * q_dec[None, :, :, None], state)
        new_state = state * chunk_dec[None, :, None, None] + jnp.einsum(
            'bhjd,bhjv->bhdv', kb * k_dec[None, :, :, None], vb)
        return new_state, inner + cross

    state0 = jnp.zeros((B, H, dk, dv), jnp.float32)
    _, out = lax.scan(body, state0, (qc, kc, vc))
    out = out.transpose(1, 0, 3, 2, 4).reshape(B, S, H, dv)
    mu = jnp.mean(out, axis=-1, keepdims=True)
    var = jnp.mean(jnp.square(out - mu), axis=-1, keepdims=True)
    out = (out - mu) * lax.rsqrt(var + GN_EPS)
    return out.reshape(B, S, D_INNER) * gn_g.astype(jnp.float32)


def stick_breaking_branch(q, k, v):
    B, S, _ = q.shape
    H, d, dv = SB_HEADS, SB_QK_DIM, SB_V_DIM
    q = q.astype(jnp.float32).reshape(B, S, H, d).transpose(0, 2, 1, 3)
    k = k.astype(jnp.float32).reshape(B, S, H, d).transpose(0, 2, 1, 3)
    v = v.astype(jnp.float32).reshape(B, S, H, dv).transpose(0, 2, 1, 3)
    scale = d ** -0.5
    outs = []
    for blk in range(S // SB_BLOCK):
        q0 = blk * SB_BLOCK
        kl = q0 + SB_BLOCK
        qb = q[:, :, q0:kl]
        kb = k[:, :, :kl]
        vb = v[:, :, :kl]
        z = jnp.einsum('bhtd,bhsd->bhts', qb, kb) * scale
        t_idx = q0 + jnp.arange(SB_BLOCK)[:, None]
        s_idx = jnp.arange(kl)[None, :]
        mask = s_idx < t_idx
        log_keep = jnp.where(mask, jax.nn.log_sigmoid(-z), 0.0)
        log_keep_next = jnp.pad(log_keep[..., 1:], ((0, 0), (0, 0), (0, 0), (0, 1)))
        log_remain = lax.cumsum(log_keep_next, axis=3, reverse=True)
        a = jnp.where(mask, jnp.exp(jax.nn.log_sigmoid(z) + log_remain), 0.0)
        outs.append(jnp.einsum('bhts,bhsv->bhtv', a, vb))
    o = jnp.concatenate(outs, axis=2)
    return o.transpose(0, 2, 1, 3).reshape(B, S, H * dv)


def setup_inputs(seed: int = 0) -> dict:
    key = jax.random.key(seed)
    ks = jax.random.split(key, 10)
    f32 = jnp.float32
    x = jax.random.normal(ks[0], (BATCH, SEQ, D_MODEL), f32)
    c = jax.random.normal(ks[1], (BATCH, D_MODEL), f32)
    w_in = jax.random.normal(ks[2], (DEPTH, D_MODEL, D_PROJ), f32) * D_MODEL ** -0.5
    w_out = jax.random.normal(ks[3], (DEPTH, D_INNER, D_MODEL), f32) * D_INNER ** -0.5
    w_mod = jax.random.normal(ks[4], (DEPTH, D_MODEL, 3 * D_MODEL), f32) * D_MODEL ** -0.5
    b_mod = jax.random.normal(ks[5], (DEPTH, 3 * D_MODEL), f32) * 0.02
    pre_norm = 1.0 + 0.02 * jax.random.normal(ks[6], (DEPTH, D_MODEL), f32)
    post_norm = 1.0 + 0.02 * jax.random.normal(ks[7], (DEPTH, D_MODEL), f32)
    ret_gn = 1.0 + 0.02 * jax.random.normal(ks[8], (N_RET_LAYERS, D_INNER), f32)
    return {"x": x, "c": c, "w_in": w_in, "w_out": w_out, "w_mod": w_mod,
            "b_mod": b_mod, "pre_norm": pre_norm, "post_norm": post_norm,
            "ret_gn": ret_gn}


def reference(x, c, w_in, w_out, w_mod, b_mod, pre_norm, post_norm, ret_gn):
    c_act = jax.nn.silu(c)
    for i in range(DEPTH):
        mod = c_act @ w_mod[i] + b_mod[i]
        shift, scale, gate = jnp.split(mod, 3, axis=-1)
        h = rms_norm(x, pre_norm[i]) * (1.0 + scale[:, None, :]) + shift[:, None, :]
        proj = h @ w_in[i]
        q, k, v, g = jnp.split(proj, [D_MODEL, 2 * D_MODEL, 2 * D_MODEL + D_INNER], axis=-1)
        if i % N_MIXERS == 0:
            o = retention_branch(q, k, v, ret_gn[i // N_MIXERS])
        else:
            o = stick_breaking_branch(q, k, v)
        y = (o.astype(x.dtype) * jax.nn.silu(g)) @ w_out[i]
        x = x + gate[:, None, :] * rms_norm(y, post_norm[i])
    return x
```

```python
import functools

import jax
import jax.numpy as jnp
from jax import lax
from jax.experimental import pallas as pl
from jax.experimental.pallas import tpu as pltpu

D_MODEL = 1024
DEPTH = 4
D_INNER = 2 * D_MODEL
RET_HEADS = 4
RET_QK = D_MODEL // RET_HEADS
RET_V = D_INNER // RET_HEADS
RET_CHUNK = 128
SB_HEADS = 16
SB_QK = D_MODEL // SB_HEADS
SB_V = D_INNER // SB_HEADS
D_PROJ = 2 * D_MODEL + 2 * D_INNER
RMS_EPS = 1e-6
GN_EPS = 1e-5
ROPE_BASE = 10000.0

LANES = 128
VMEM_LIMIT = 56 * 1024 * 1024

ROW_TILE = 512
N_TILE = 512
RET_ROWS = 512
SB_TQ = 256
SB_TK = 256

BF16 = jnp.bfloat16
F32 = jnp.float32


def _silu(v):
    return v / (1.0 + jnp.exp(-v))


def _split_bf16(v):
    hi = v.astype(BF16)
    lo = (v - hi.astype(F32)).astype(BF16)
    return hi, lo


def _mod_kernel(c_ref, w_ref, b_ref, o_ref):
    a_hi, a_lo = _split_bf16(_silu(c_ref[...]))
    w_hi, w_lo = _split_bf16(w_ref[0])
    acc = jnp.dot(a_hi, w_hi, preferred_element_type=F32)
    acc += jnp.dot(a_hi, w_lo, preferred_element_type=F32)
    acc += jnp.dot(a_lo, w_hi, preferred_element_type=F32)
    o_ref[0] = acc + b_ref[0]


def _modulation(c, w_mod, b_mod):
    B = c.shape[0]
    return pl.pallas_call(
        _mod_kernel,
        grid=(DEPTH, 3),
        in_specs=[
            pl.BlockSpec((B, D_MODEL), lambda i, j: (0, 0)),
            pl.BlockSpec((1, D_MODEL, D_MODEL), lambda i, j: (i, 0, j)),
            pl.BlockSpec((1, 1, D_MODEL), lambda i, j: (i, 0, j)),
        ],
        out_specs=pl.BlockSpec((1, B, D_MODEL), lambda i, j: (i, 0, j)),
        out_shape=jax.ShapeDtypeStruct((DEPTH, B, 3 * D_MODEL), F32),
        compiler_params=pltpu.CompilerParams(
            dimension_semantics=("arbitrary", "arbitrary"),
            vmem_limit_bytes=VMEM_LIMIT),
        name="adaln_modulation",
    )(c, w_mod, b_mod.reshape(DEPTH, 1, 3 * D_MODEL))


def _inproj_kernel(x_ref, g_ref, shift_ref, scale_ref, w_ref, o_ref):
    xf = x_ref[...]
    ms = jnp.mean(xf * xf, axis=-1, keepdims=True)
    y = xf * lax.rsqrt(ms + RMS_EPS) * g_ref[...]
    h = (y * (1.0 + scale_ref[0]) + shift_ref[0]).astype(BF16)
    for n in range(D_PROJ // N_TILE):
        cols = slice(n * N_TILE, (n + 1) * N_TILE)
        o_ref[:, cols] = jnp.dot(
            h, w_ref[:, cols], preferred_element_type=F32).astype(BF16)


def _in_projection(x2d, pre_g, shift, scale, w_bf16, seq):
    M = x2d.shape[0]
    steps_per_batch = seq // ROW_TILE
    return pl.pallas_call(
        _inproj_kernel,
        grid=(M // ROW_TILE,),
        in_specs=[
            pl.BlockSpec((ROW_TILE, D_MODEL), lambda i: (i, 0)),
            pl.BlockSpec((1, D_MODEL), lambda i: (0, 0)),
            pl.BlockSpec((1, 1, D_MODEL), lambda i: (i // steps_per_batch, 0, 0)),
            pl.BlockSpec((1, 1, D_MODEL), lambda i: (i // steps_per_batch, 0, 0)),
            pl.BlockSpec((D_MODEL, D_PROJ), lambda i: (0, 0),
                         pipeline_mode=pl.Buffered(1)),
        ],
        out_specs=pl.BlockSpec((ROW_TILE, D_PROJ), lambda i: (i, 0)),
        out_shape=jax.ShapeDtypeStruct((M, D_PROJ), BF16),
        compiler_params=pltpu.CompilerParams(
            dimension_semantics=("arbitrary",),
            vmem_limit_bytes=VMEM_LIMIT),
        name="norm_inproj",
    )(x2d, pre_g, shift, scale, w_bf16)


def _outproj_kernel(yg_ref, w_ref, x_ref, g_ref, gate_ref, o_ref):
    y = jnp.dot(yg_ref[...], w_ref[...], preferred_element_type=F32)
    ms = jnp.mean(y * y, axis=-1, keepdims=True)
    yn = y * lax.rsqrt(ms + RMS_EPS) * g_ref[...]
    o_ref[...] = x_ref[...] + gate_ref[0] * yn


def _out_projection(yg2d, w_bf16, x2d, post_g, gate, seq):
    M = x2d.shape[0]
    steps_per_batch = seq // ROW_TILE
    return pl.pallas_call(
        _outproj_kernel,
        grid=(M // ROW_TILE,),
        in_specs=[
            pl.BlockSpec((ROW_TILE, D_INNER), lambda i: (i, 0)),
            pl.BlockSpec((D_INNER, D_MODEL), lambda i: (0, 0),
                         pipeline_mode=pl.Buffered(1)),
            pl.BlockSpec((ROW_TILE, D_MODEL), lambda i: (i, 0)),
            pl.BlockSpec((1, D_MODEL), lambda i: (0, 0)),
            pl.BlockSpec((1, 1, D_MODEL), lambda i: (i // steps_per_batch, 0, 0)),
        ],
        out_specs=pl.BlockSpec((ROW_TILE, D_MODEL), lambda i: (i, 0)),
        out_shape=jax.ShapeDtypeStruct((M, D_MODEL), F32),
        input_output_aliases={2: 0},
        compiler_params=pltpu.CompilerParams(
            dimension_semantics=("arbitrary",),
            vmem_limit_bytes=VMEM_LIMIT),
        name="outproj_residual",
    )(yg2d, w_bf16, x2d, post_g, gate)


def _retention_tables(seq):
    H, dk, C = RET_HEADS, RET_QK, RET_CHUNK
    omega = 1.0 / (ROPE_BASE ** jnp.linspace(0.0, 1.0, dk // 2, dtype=F32))
    ang = jnp.arange(seq, dtype=F32)[:, None] * omega[None, :]
    cos, sin = jnp.cos(ang), jnp.sin(ang)
    log_gamma = jnp.log1p(-jnp.exp2(jnp.linspace(-5.0, -9.0, H, dtype=F32)))
    idx = jnp.arange(C, dtype=F32)
    diff = idx[:, None] - idx[None, :]
    decay_mat = jnp.where(
        diff[None] >= 0,
        jnp.exp(jnp.maximum(diff, 0.0)[None] * log_gamma[:, None, None]), 0.0)
    q_dec = jnp.exp((idx[None, :] + 1.0) * log_gamma[:, None])
    k_dec = jnp.exp((C - 1.0 - idx[None, :]) * log_gamma[:, None])
    chunk_dec = jnp.exp(C * log_gamma)
    q_dec = jnp.broadcast_to(q_dec[:, :, None], (H, C, dk))
    k_dec = jnp.broadcast_to(k_dec[:, :, None], (H, C, dk))
    chunk_dec = jnp.broadcast_to(chunk_dec[:, None, None], (H, 1, RET_V))
    return cos, sin, decay_mat, q_dec, k_dec, chunk_dec


def _rotate(t, cos, sin):
    half = RET_QK // 2
    t1, t2 = t[:, :half], t[:, half:]
    return jnp.concatenate([t1 * cos - t2 * sin, t2 * cos + t1 * sin], axis=-1)


def _retention_kernel(q_ref, k_ref, v_ref, g_ref, cos_ref, sin_ref, dm_ref,
                      qd_ref, kd_ref, cd_ref, gn_ref, o_ref, state_ref):
    @pl.when(pl.program_id(2) == 0)
    def _():
        state_ref[...] = jnp.zeros_like(state_ref)

    C = RET_CHUNK
    for c in range(RET_ROWS // C):
        rows = slice(c * C, (c + 1) * C)
        cos, sin = cos_ref[rows, :], sin_ref[rows, :]
        q = _rotate(q_ref[0, rows, :].astype(F32), cos, sin)
        k = _rotate(k_ref[0, rows, :].astype(F32), cos, sin) * (RET_QK ** -0.5)
        v = v_ref[0, rows, :]
        state = state_ref[...]
        scores = lax.dot_general(
            q.astype(BF16), k.astype(BF16), (((1,), (1,)), ((), ())),
            preferred_element_type=F32) * dm_ref[0]
        inner = jnp.dot(scores.astype(BF16), v, preferred_element_type=F32)
        cross = jnp.dot((q * qd_ref[0]).astype(BF16), state.astype(BF16),
                        preferred_element_type=F32)
        kv = lax.dot_general(
            (k * kd_ref[0]).astype(BF16), v, (((0,), (0,)), ((), ())),
            preferred_element_type=F32)
        state_ref[...] = state * cd_ref[0] + kv
        o = inner + cross
        mu = jnp.mean(o, axis=-1, keepdims=True)
        d = o - mu
        var = jnp.mean(d * d, axis=-1, keepdims=True)
        on = d * lax.rsqrt(var + GN_EPS) * gn_ref[...]
        o_ref[0, rows, :] = (on * _silu(g_ref[0, rows, :].astype(F32))).astype(BF16)


def _retention(proj3d, gn_g, tables):
    B, S, _ = proj3d.shape
    cos, sin, decay_mat, q_dec, k_dec, chunk_dec = tables
    H, dk, dv, C = RET_HEADS, RET_QK, RET_V, RET_CHUNK
    k_off = D_MODEL // dk
    v_off = 2 * D_MODEL // dv
    g_off = (2 * D_MODEL + D_INNER) // dv
    return pl.pallas_call(
        _retention_kernel,
        grid=(B, H, S // RET_ROWS),
        in_specs=[
            pl.BlockSpec((1, RET_ROWS, dk), lambda b, h, s: (b, s, h)),
            pl.BlockSpec((1, RET_ROWS, dk), lambda b, h, s: (b, s, k_off + h)),
            pl.BlockSpec((1, RET_ROWS, dv), lambda b, h, s: (b, s, v_off + h)),
            pl.BlockSpec((1, RET_ROWS, dv), lambda b, h, s: (b, s, g_off + h)),
            pl.BlockSpec((RET_ROWS, dk // 2), lambda b, h, s: (s, 0)),
            pl.BlockSpec((RET_ROWS, dk // 2), lambda b, h, s: (s, 0)),
            pl.BlockSpec((1, C, C), lambda b, h, s: (h, 0, 0)),
            pl.BlockSpec((1, C, dk), lambda b, h, s: (h, 0, 0)),
            pl.BlockSpec((1, C, dk), lambda b, h, s: (h, 0, 0)),
            pl.BlockSpec((1, 1, dv), lambda b, h, s: (h, 0, 0)),
            pl.BlockSpec((1, dv), lambda b, h, s: (0, h)),
        ],
        out_specs=pl.BlockSpec((1, RET_ROWS, dv), lambda b, h, s: (b, s, h)),
        out_shape=jax.ShapeDtypeStruct((B, S, D_INNER), BF16),
        scratch_shapes=[pltpu.VMEM((dk, dv), F32)],
        compiler_params=pltpu.CompilerParams(
            dimension_semantics=("arbitrary", "arbitrary", "arbitrary"),
            vmem_limit_bytes=VMEM_LIMIT),
        name="retention_mixer",
    )(proj3d, proj3d, proj3d, proj3d, cos, sin, decay_mat, q_dec, k_dec,
      chunk_dec, gn_g)


def _softplus(z):
    return jnp.maximum(z, 0.0) + jnp.log(1.0 + jnp.exp(-jnp.abs(z)))


def _sb_tile(q, k_tile, v_tile, tri, r_prev, mask):
    z = lax.dot_general(q, k_tile, (((1,), (1,)), ((), ())),
                        preferred_element_type=F32)
    sp = _softplus(z)
    if mask is not None:
        sp = jnp.where(mask, sp, 0.0)
    after = jnp.dot(sp.astype(BF16), tri, preferred_element_type=F32)
    a = jnp.exp(z - sp - after - r_prev)
    if mask is not None:
        a = jnp.where(mask, a, 0.0)
    contrib = jnp.dot(a.astype(BF16), v_tile, preferred_element_type=F32)
    return contrib, r_prev + jnp.sum(sp, axis=-1, keepdims=True)


def _sb_kernel(q_ref, k_ref, v_ref, g_ref, o_ref, acc_ref, r_ref):
    qi = pl.program_id(2)
    lane = lax.broadcasted_iota(jnp.int32, (SB_TQ, LANES), 1)
    qf = q_ref[0].astype(F32) * (SB_QK ** -0.5)
    q_heads = [jnp.where(lane < SB_QK, qf, 0.0).astype(BF16),
               jnp.where(lane >= SB_QK, qf, 0.0).astype(BF16)]
    row = lax.broadcasted_iota(jnp.int32, (SB_TK, SB_TK), 0)
    col = lax.broadcasted_iota(jnp.int32, (SB_TK, SB_TK), 1)
    tri = jnp.where(row > col, 1.0, 0.0).astype(BF16)
    diag_mask = col < row

    def tiles(kb):
        start = pl.multiple_of(kb * SB_TK, SB_TK)
        k_tile = k_ref[0, pl.ds(start, SB_TK), :]
        v_tile = v_ref[0, pl.ds(start, SB_TK), :]
        return k_tile, v_tile

    k_tile, v_tile = tiles(qi)
    for h in range(2):
        contrib, r = _sb_tile(q_heads[h], k_tile,
                              v_tile[:, h * SB_V:(h + 1) * SB_V], tri,
                              jnp.zeros((SB_TQ, 1), F32), diag_mask)
        acc_ref[h] = contrib
        r_ref[h] = r

    def body(i, carry):
        k_tile, v_tile = tiles(qi - 1 - i)
        for h in range(2):
            contrib, r = _sb_tile(q_heads[h], k_tile,
                                  v_tile[:, h * SB_V:(h + 1) * SB_V], tri,
                                  r_ref[h], None)
            acc_ref[h] += contrib
            r_ref[h] = r
        return carry

    lax.fori_loop(0, qi, body, 0)

    for h in range(2):
        cols = slice(h * SB_V, (h + 1) * SB_V)
        gate = _silu(g_ref[0, :, cols].astype(F32))
        o_ref[0, :, cols] = (acc_ref[h] * gate).astype(BF16)


def _stick_breaking(proj3d):
    B, S, _ = proj3d.shape
    pair_v = 2 * SB_V
    k_off = D_MODEL // LANES
    v_off = 2 * D_MODEL // pair_v
    g_off = (2 * D_MODEL + D_INNER) // pair_v
    return pl.pallas_call(
        _sb_kernel,
        grid=(B, SB_HEADS // 2, S // SB_TQ),
        in_specs=[
            pl.BlockSpec((1, SB_TQ, LANES), lambda b, p, i: (b, i, p)),
            pl.BlockSpec((1, S, LANES), lambda b, p, i: (b, 0, k_off + p)),
            pl.BlockSpec((1, S, pair_v), lambda b, p, i: (b, 0, v_off + p)),
            pl.BlockSpec((1, SB_TQ, pair_v), lambda b, p, i: (b, i, g_off + p)),
        ],
        out_specs=pl.BlockSpec((1, SB_TQ, pair_v), lambda b, p, i: (b, i, p)),
        out_shape=jax.ShapeDtypeStruct((B, S, D_INNER), BF16),
        scratch_shapes=[pltpu.VMEM((2, SB_TQ, SB_V), F32),
                        pltpu.VMEM((2, SB_TQ, 1), F32)],
        compiler_params=pltpu.CompilerParams(
            dimension_semantics=("arbitrary", "arbitrary", "arbitrary"),
            vmem_limit_bytes=VMEM_LIMIT),
        name="stick_breaking_mixer",
    )(proj3d, proj3d, proj3d, proj3d)


def kernel(x, c, w_in, w_out, w_mod, b_mod, pre_norm, post_norm, ret_gn):
    B, S, D = x.shape
    assert D == D_MODEL and S % ROW_TILE == 0 and S % RET_ROWS == 0
    assert S % SB_TQ == 0 and SB_TQ == SB_TK
    mod = _modulation(c, w_mod, b_mod)
    w_in_b = w_in.astype(BF16)
    w_out_b = w_out.astype(BF16)
    tables = _retention_tables(S)
    x2d = x.reshape(B * S, D)
    for i in range(DEPTH):
        shift = mod[i, :, :D].reshape(B, 1, D)
        scale = mod[i, :, D:2 * D].reshape(B, 1, D)
        gate = mod[i, :, 2 * D:].reshape(B, 1, D)
        proj = _in_projection(x2d, pre_norm[i].reshape(1, D), shift, scale,
                              w_in_b[i], S)
        proj3d = proj.reshape(B, S, D_PROJ)
        if i % 2 == 0:
            yg = _retention(proj3d, ret_gn[i // 2].reshape(1, D_INNER), tables)
        else:
            yg = _stick_breaking(proj3d)
        x2d = _out_projection(yg.reshape(B * S, D_INNER), w_out_b[i], x2d,
                              post_norm[i].reshape(1, D), gate, S)
    return x2d.reshape(B, S, D)
```

```python
import functools

import jax
import jax.numpy as jnp
from jax import lax
from jax.experimental import pallas as pl
from jax.experimental.pallas import tpu as pltpu

D_MODEL = 1024
DEPTH = 4
D_INNER = 2 * D_MODEL
RET_HEADS = 4
RET_QK = D_MODEL // RET_HEADS
RET_V = D_INNER // RET_HEADS
RET_CHUNK = 128
SB_HEADS = 16
SB_QK = D_MODEL // SB_HEADS
SB_V = D_INNER // SB_HEADS
D_PROJ = 2 * D_MODEL + 2 * D_INNER
RMS_EPS = 1e-6
GN_EPS = 1e-5
ROPE_BASE = 10000.0

LANES = 128
VMEM_LIMIT = 56 * 1024 * 1024

ROW_TILE = 512
N_TILE = 512
RET_ROWS = 512
SB_TQ = 256
SB_TK = 256

BF16 = jnp.bfloat16
F32 = jnp.float32


def _silu(v):
    return v / (1.0 + jnp.exp(-v))


def _split_bf16(v):
    hi = v.astype(BF16)
    lo = (v - hi.astype(F32)).astype(BF16)
    return hi, lo


def _mod_kernel(c_ref, w_ref, b_ref, o_ref):
    a_hi, a_lo = _split_bf16(_silu(c_ref[...]))
    w_hi, w_lo = _split_bf16(w_ref[0])
    acc = jnp.dot(a_hi, w_hi, preferred_element_type=F32)
    acc += jnp.dot(a_hi, w_lo, preferred_element_type=F32)
    acc += jnp.dot(a_lo, w_hi, preferred_element_type=F32)
    o_ref[0] = acc + b_ref[0]


def _modulation(c, w_mod, b_mod):
    B = c.shape[0]
    return pl.pallas_call(
        _mod_kernel,
        grid=(DEPTH, 3),
        in_specs=[
            pl.BlockSpec((B, D_MODEL), lambda i, j: (0, 0)),
            pl.BlockSpec((1, D_MODEL, D_MODEL), lambda i, j: (i, 0, j)),
            pl.BlockSpec((1, 1, D_MODEL), lambda i, j: (i, 0, j)),
        ],
        out_specs=pl.BlockSpec((1, B, D_MODEL), lambda i, j: (i, 0, j)),
        out_shape=jax.ShapeDtypeStruct((DEPTH, B, 3 * D_MODEL), F32),
        compiler_params=pltpu.CompilerParams(
            dimension_semantics=("arbitrary", "arbitrary"),
            vmem_limit_bytes=VMEM_LIMIT),
        name="adaln_modulation",
    )(c, w_mod, b_mod.reshape(DEPTH, 1, 3 * D_MODEL))


def _inproj_kernel(x_ref, g_ref, shift_ref, scale_ref, w_ref, o_ref):
    xf = x_ref[...]
    ms = jnp.mean(xf * xf, axis=-1, keepdims=True)
    y = xf * lax.rsqrt(ms + RMS_EPS) * g_ref[...]
    h = (y * (1.0 + scale_ref[0]) + shift_ref[0]).astype(BF16)
    for n in range(D_PROJ // N_TILE):
        cols = slice(n * N_TILE, (n + 1) * N_TILE)
        o_ref[:, cols] = jnp.dot(
            h, w_ref[:, cols], preferred_element_type=F32).astype(BF16)


def _in_projection(x2d, pre_g, shift, scale, w_bf16, seq):
    M = x2d.shape[0]
    steps_per_batch = seq // ROW_TILE
    return pl.pallas_call(
        _inproj_kernel,
        grid=(M // ROW_TILE,),
        in_specs=[
            pl.BlockSpec((ROW_TILE, D_MODEL), lambda i: (i, 0)),
            pl.BlockSpec((1, D_MODEL), lambda i: (0, 0)),
            pl.BlockSpec((1, 1, D_MODEL), lambda i: (i // steps_per_batch, 0, 0)),
            pl.BlockSpec((1, 1, D_MODEL), lambda i: (i // steps_per_batch, 0, 0)),
            pl.BlockSpec((D_MODEL, D_PROJ), lambda i: (0, 0),
                         pipeline_mode=pl.Buffered(1)),
        ],
        out_specs=pl.BlockSpec((ROW_TILE, D_PROJ), lambda i: (i, 0)),
        out_shape=jax.ShapeDtypeStruct((M, D_PROJ), BF16),
        compiler_params=pltpu.CompilerParams(
            dimension_semantics=("arbitrary",),
            vmem_limit_bytes=VMEM_LIMIT),
        name="norm_inproj",
    )(x2d, pre_g, shift, scale, w_bf16)


def _outproj_kernel(yg_ref, w_ref, x_ref, g_ref, gate_ref, o_ref):
    y = jnp.dot(yg_ref[...], w_ref[...], preferred_element_type=F32)
    ms = jnp.mean(y * y, axis=-1, keepdims=True)
    yn = y * lax.rsqrt(ms + RMS_EPS) * g_ref[...]
    o_ref[...] = x_ref[...] + gate_ref[0] * yn


def _out_projection(yg2d, w_bf16, x2d, post_g, gate, seq):
    M = x2d.shape[0]
    steps_per_batch = seq // ROW_TILE
    return pl.pallas_call(
        _outproj_kernel,
        grid=(M // ROW_TILE,),
        in_specs=[
            pl.BlockSpec((ROW_TILE, D_INNER), lambda i: (i, 0)),
            pl.BlockSpec((D_INNER, D_MODEL), lambda i: (0, 0),
                         pipeline_mode=pl.Buffered(1)),
            pl.BlockSpec((ROW_TILE, D_MODEL), lambda i: (i, 0)),
            pl.BlockSpec((1, D_MODEL), lambda i: (0, 0)),
            pl.BlockSpec((1, 1, D_MODEL), lambda i: (i // steps_per_batch, 0, 0)),
        ],
        out_specs=pl.BlockSpec((ROW_TILE, D_MODEL), lambda i: (i, 0)),
        out_shape=jax.ShapeDtypeStruct((M, D_MODEL), F32),
        input_output_aliases={2: 0},
        compiler_params=pltpu.CompilerParams(
            dimension_semantics=("arbitrary",),
            vmem_limit_bytes=VMEM_LIMIT),
        name="outproj_residual",
    )(yg2d, w_bf16, x2d, post_g, gate)


def _retention_tables(seq):
    H, dk, C = RET_HEADS, RET_QK, RET_CHUNK
    omega = 1.0 / (ROPE_BASE ** jnp.linspace(0.0, 1.0, dk // 2, dtype=F32))
    ang = jnp.arange(seq, dtype=F32)[:, None] * omega[None, :]
    cos, sin = jnp.cos(ang), jnp.sin(ang)
    log_gamma = jnp.log1p(-jnp.exp2(jnp.linspace(-5.0, -9.0, H, dtype=F32)))
    idx = jnp.arange(C, dtype=F32)
    diff = idx[:, None] - idx[None, :]
    decay_mat = jnp.where(
        diff[None] >= 0,
        jnp.exp(jnp.maximum(diff, 0.0)[None] * log_gamma[:, None, None]), 0.0)
    q_dec = jnp.exp((idx[None, :] + 1.0) * log_gamma[:, None])
    k_dec = jnp.exp((C - 1.0 - idx[None, :]) * log_gamma[:, None])
    chunk_dec = jnp.exp(C * log_gamma)
    q_dec = jnp.broadcast_to(q_dec[:, :, None], (H, C, dk))
    k_dec = jnp.broadcast_to(k_dec[:, :, None], (H, C, dk))
    chunk_dec = jnp.broadcast_to(chunk_dec[:, None, None], (H, 1, RET_V))
    return cos, sin, decay_mat, q_dec, k_dec, chunk_dec


def _rotate(t, cos, sin):
    half = RET_QK // 2
    t1, t2 = t[:, :half], t[:, half:]
    return jnp.concatenate([t1 * cos - t2 * sin, t2 * cos + t1 * sin], axis=-1)


def _retention_kernel(q_ref, k_ref, v_ref, g_ref, cos_ref, sin_ref, dm_ref,
                      qd_ref, kd_ref, cd_ref, gn_ref, o_ref, state_ref):
    @pl.when(pl.program_id(2) == 0)
    def _():
        state_ref[...] = jnp.zeros_like(state_ref)

    C = RET_CHUNK
    for c in range(RET_ROWS // C):
        rows = slice(c * C, (c + 1) * C)
        cos, sin = cos_ref[rows, :], sin_ref[rows, :]
        q = _rotate(q_ref[0, rows, :].astype(F32), cos, sin)
        k = _rotate(k_ref[0, rows, :].astype(F32), cos, sin) * (RET_QK ** -0.5)
        v = v_ref[0, rows, :]
        state = state_ref[...]
        scores = lax.dot_general(
            q.astype(BF16), k.astype(BF16), (((1,), (1,)), ((), ())),
            preferred_element_type=F32) * dm_ref[0]
        inner = jnp.dot(scores.astype(BF16), v, preferred_element_type=F32)
        cross = jnp.dot((q * qd_ref[0]).astype(BF16), state.astype(BF16),
                        preferred_element_type=F32)
        kv = lax.dot_general(
            (k * kd_ref[0]).astype(BF16), v, (((0,), (0,)), ((), ())),
            preferred_element_type=F32)
        state_ref[...] = state * cd_ref[0] + kv
        o = inner + cross
        mu = jnp.mean(o, axis=-1, keepdims=True)
        d = o - mu
        var = jnp.mean(d * d, axis=-1, keepdims=True)
        on = d * lax.rsqrt(var + GN_EPS) * gn_ref[...]
        o_ref[0, rows, :] = (on * _silu(g_ref[0, rows, :].astype(F32))).astype(BF16)


def _retention(proj3d, gn_g, tables):
    B, S, _ = proj3d.shape
    cos, sin, decay_mat, q_dec, k_dec, chunk_dec = tables
    H, dk, dv, C = RET_HEADS, RET_QK, RET_V, RET_CHUNK
    k_off = D_MODEL // dk
    v_off = 2 * D_MODEL // dv
    g_off = (2 * D_MODEL + D_INNER) // dv
    return pl.pallas_call(
        _retention_kernel,
        grid=(B, H, S // RET_ROWS),
        in_specs=[
            pl.BlockSpec((1, RET_ROWS, dk), lambda b, h, s: (b, s, h)),
            pl.BlockSpec((1, RET_ROWS, dk), lambda b, h, s: (b, s, k_off + h)),
            pl.BlockSpec((1, RET_ROWS, dv), lambda b, h, s: (b, s, v_off + h)),
            pl.BlockSpec((1, RET_ROWS, dv), lambda b, h, s: (b, s, g_off + h)),
            pl.BlockSpec((RET_ROWS, dk // 2), lambda b, h, s: (s, 0)),
            pl.BlockSpec((RET_ROWS, dk // 2), lambda b, h, s: (s, 0)),
            pl.BlockSpec((1, C, C), lambda b, h, s: (h, 0, 0)),
            pl.BlockSpec((1, C, dk), lambda b, h, s: (h, 0, 0)),
            pl.BlockSpec((1, C, dk), lambda b, h, s: (h, 0, 0)),
            pl.BlockSpec((1, 1, dv), lambda b, h, s: (h, 0, 0)),
            pl.BlockSpec((1, dv), lambda b, h, s: (0, h)),
        ],
        out_specs=pl.BlockSpec((1, RET_ROWS, dv), lambda b, h, s: (b, s, h)),
        out_shape=jax.ShapeDtypeStruct((B, S, D_INNER), BF16),
        scratch_shapes=[pltpu.VMEM((dk, dv), F32)],
        compiler_params=pltpu.CompilerParams(
            dimension_semantics=("arbitrary", "arbitrary", "arbitrary"),
            vmem_limit_bytes=VMEM_LIMIT),
        name="retention_mixer",
    )(proj3d, proj3d, proj3d, proj3d, cos, sin, decay_mat, q_dec, k_dec,
      chunk_dec, gn_g)


LOG2E = 1.4426950408889634


SB_EXIT_MASS = 152.0


def _sb_tile_terms(q, k_tile, tri, mask):
    z = lax.dot_general(q, k_tile, (((1,), (1,)), ((), ())),
                        preferred_element_type=F32)
    sp = jnp.maximum(z, 0.0) + jnp.log2(1.0 + jnp.exp2(-jnp.abs(z)))
    if mask is not None:
        sp = jnp.where(mask, sp, 0.0)
    after = jnp.dot(sp.astype(BF16), tri, preferred_element_type=F32)
    return z - sp - after, jnp.sum(sp, axis=-1, keepdims=True)


def _sb_weights(base, r_prev, mask):
    a = jnp.exp2(base - r_prev)
    if mask is not None:
        a = jnp.where(mask, a, 0.0)
    return a.astype(BF16)


def _sb_kernel(q_ref, k_ref, v_ref, g_ref, o_ref, acc_ref, r_ref):
    qi = pl.program_id(2)
    lane = lax.broadcasted_iota(jnp.int32, (SB_TQ, LANES), 1)
    qf = q_ref[0].astype(F32) * (SB_QK ** -0.5 * LOG2E)
    q_heads = [jnp.where(lane < SB_QK, qf, 0.0).astype(BF16),
               jnp.where(lane >= SB_QK, qf, 0.0).astype(BF16)]
    row = lax.broadcasted_iota(jnp.int32, (SB_TK, SB_TK), 0)
    col = lax.broadcasted_iota(jnp.int32, (SB_TK, SB_TK), 1)
    tri = jnp.where(row > col, 1.0, 0.0).astype(BF16)
    diag_mask = col < row

    def two_tiles(kb_near, kb_far, near_mask, first):
        far_valid = kb_far >= 0
        starts = [pl.multiple_of(kb_near * SB_TK, SB_TK),
                  pl.multiple_of(jnp.maximum(kb_far, 0) * SB_TK, SB_TK)]
        k_tiles = [k_ref[0, pl.ds(s, SB_TK), :] for s in starts]
        v_tiles = [v_ref[0, pl.ds(s, SB_TK), :] for s in starts]
        r_min = []
        for h in range(2):
            vcols = slice(h * SB_V, (h + 1) * SB_V)
            base_n, rs_n = _sb_tile_terms(q_heads[h], k_tiles[0], tri, near_mask)
            base_f, rs_f = _sb_tile_terms(q_heads[h], k_tiles[1], tri, None)
            r0 = jnp.zeros((SB_TQ, 1), F32) if first else r_ref[h]
            r1 = r0 + rs_n
            a_n = _sb_weights(base_n, r0, near_mask)
            a_f = _sb_weights(base_f, r1, None)
            c_n = jnp.dot(a_n, v_tiles[0][:, vcols], preferred_element_type=F32)
            c_f = jnp.dot(a_f, v_tiles[1][:, vcols], preferred_element_type=F32)
            contrib = c_n + jnp.where(far_valid, c_f, 0.0)
            if first:
                acc_ref[h] = contrib
            else:
                acc_ref[h] += contrib
            r2 = r1 + jnp.where(far_valid, rs_f, 0.0)
            r_ref[h] = r2
            r_min.append(jnp.min(r2))
        return jnp.minimum(r_min[0], r_min[1])

    mass0 = two_tiles(qi, qi - 1, diag_mask, True)

    def more_tiles(carry):
        j, mass = carry
        return jnp.logical_and(j < qi // 2, mass < SB_EXIT_MASS)

    def body(carry):
        j, _ = carry
        return j + 1, two_tiles(qi - 2 - 2 * j, qi - 3 - 2 * j, None, False)

    lax.while_loop(more_tiles, body, (jnp.int32(0), mass0))

    for h in range(2):
        cols = slice(h * SB_V, (h + 1) * SB_V)
        gate = _silu(g_ref[0, :, cols].astype(F32))
        o_ref[0, :, cols] = (acc_ref[h] * gate).astype(BF16)


def _stick_breaking(proj3d):
    B, S, _ = proj3d.shape
    pair_v = 2 * SB_V
    k_off = D_MODEL // LANES
    v_off = 2 * D_MODEL // pair_v
    g_off = (2 * D_MODEL + D_INNER) // pair_v
    return pl.pallas_call(
        _sb_kernel,
        grid=(B, SB_HEADS // 2, S // SB_TQ),
        in_specs=[
            pl.BlockSpec((1, SB_TQ, LANES), lambda b, p, i: (b, i, p)),
            pl.BlockSpec((1, S, LANES), lambda b, p, i: (b, 0, k_off + p)),
            pl.BlockSpec((1, S, pair_v), lambda b, p, i: (b, 0, v_off + p)),
            pl.BlockSpec((1, SB_TQ, pair_v), lambda b, p, i: (b, i, g_off + p)),
        ],
        out_specs=pl.BlockSpec((1, SB_TQ, pair_v), lambda b, p, i: (b, i, p)),
        out_shape=jax.ShapeDtypeStruct((B, S, D_INNER), BF16),
        scratch_shapes=[pltpu.VMEM((2, SB_TQ, SB_V), F32),
                        pltpu.VMEM((2, SB_TQ, 1), F32)],
        compiler_params=pltpu.CompilerParams(
            dimension_semantics=("arbitrary", "arbitrary", "arbitrary"),
            vmem_limit_bytes=VMEM_LIMIT),
        name="stick_breaking_mixer",
    )(proj3d, proj3d, proj3d, proj3d)


def kernel(x, c, w_in, w_out, w_mod, b_mod, pre_norm, post_norm, ret_gn):
    B, S, D = x.shape
    assert D == D_MODEL and S % ROW_TILE == 0 and S % RET_ROWS == 0
    assert S % SB_TQ == 0 and SB_TQ == SB_TK
    mod = _modulation(c, w_mod, b_mod)
    w_in_b = w_in.astype(BF16)
    w_out_b = w_out.astype(BF16)
    tables = _retention_tables(S)
    x2d = x.reshape(B * S, D)
    for i in range(DEPTH):
        shift = mod[i, :, :D].reshape(B, 1, D)
        scale = mod[i, :, D:2 * D].reshape(B, 1, D)
        gate = mod[i, :, 2 * D:].reshape(B, 1, D)
        proj = _in_projection(x2d, pre_norm[i].reshape(1, D), shift, scale,
                              w_in_b[i], S)
        proj3d = proj.reshape(B, S, D_PROJ)
        if i % 2 == 0:
            yg = _retention(proj3d, ret_gn[i // 2].reshape(1, D_INNER), tables)
        else:
            yg = _stick_breaking(proj3d)
        x2d = _out_projection(yg.reshape(B * S, D_INNER), w_out_b[i], x2d,
                              post_norm[i].reshape(1, D), gate, S)
    return x2d.reshape(B, S, D)
```

```python
import functools

import jax
import jax.numpy as jnp
from jax import lax
from jax.experimental import pallas as pl
from jax.experimental.pallas import tpu as pltpu

D_MODEL = 1024
DEPTH = 4
D_INNER = 2 * D_MODEL
RET_HEADS = 4
RET_QK = D_MODEL // RET_HEADS
RET_V = D_INNER // RET_HEADS
RET_CHUNK = 256
SB_HEADS = 16
SB_QK = D_MODEL // SB_HEADS
SB_V = D_INNER // SB_HEADS
D_PROJ = 2 * D_MODEL + 2 * D_INNER
RMS_EPS = 1e-6
GN_EPS = 1e-5
ROPE_BASE = 10000.0

LANES = 128
VMEM_LIMIT = 56 * 1024 * 1024

ROW_TILE = 512
N_TILE = 512
RET_ROWS = 1024
SB_TQ = 256
SB_TK = 256
SB_FIRST_TILES = 3
SB_LOOP_TILES = 3

BF16 = jnp.bfloat16
F32 = jnp.float32


def _silu(v):
    return v / (1.0 + jnp.exp(-v))


def _split_bf16(v):
    hi = v.astype(BF16)
    lo = (v - hi.astype(F32)).astype(BF16)
    return hi, lo


def _mod_kernel(c_ref, w_ref, b_ref, o_ref):
    a_hi, a_lo = _split_bf16(_silu(c_ref[...]))
    w_hi, w_lo = _split_bf16(w_ref[0])
    acc = jnp.dot(a_hi, w_hi, preferred_element_type=F32)
    acc += jnp.dot(a_hi, w_lo, preferred_element_type=F32)
    acc += jnp.dot(a_lo, w_hi, preferred_element_type=F32)
    o_ref[0] = acc + b_ref[0]


def _modulation(c, w_mod, b_mod):
    B = c.shape[0]
    return pl.pallas_call(
        _mod_kernel,
        grid=(DEPTH, 3),
        in_specs=[
            pl.BlockSpec((B, D_MODEL), lambda i, j: (0, 0)),
            pl.BlockSpec((1, D_MODEL, D_MODEL), lambda i, j: (i, 0, j)),
            pl.BlockSpec((1, 1, D_MODEL), lambda i, j: (i, 0, j)),
        ],
        out_specs=pl.BlockSpec((1, B, D_MODEL), lambda i, j: (i, 0, j)),
        out_shape=jax.ShapeDtypeStruct((DEPTH, B, 3 * D_MODEL), F32),
        compiler_params=pltpu.CompilerParams(
            dimension_semantics=("arbitrary", "arbitrary"),
            vmem_limit_bytes=VMEM_LIMIT),
        name="adaln_modulation",
    )(c, w_mod, b_mod.reshape(DEPTH, 1, 3 * D_MODEL))


def _inproj_kernel(x_ref, g_ref, shift_ref, scale_ref, w_ref, o_ref):
    xf = x_ref[...]
    ms = jnp.mean(xf * xf, axis=-1, keepdims=True)
    y = xf * lax.rsqrt(ms + RMS_EPS) * g_ref[...]
    h = (y * (1.0 + scale_ref[0]) + shift_ref[0]).astype(BF16)
    for n in range(D_PROJ // N_TILE):
        cols = slice(n * N_TILE, (n + 1) * N_TILE)
        o_ref[:, cols] = jnp.dot(
            h, w_ref[:, cols], preferred_element_type=F32).astype(BF16)


def _in_projection(x2d, pre_g, shift, scale, w_bf16, seq):
    M = x2d.shape[0]
    steps_per_batch = seq // ROW_TILE
    return pl.pallas_call(
        _inproj_kernel,
        grid=(M // ROW_TILE,),
        in_specs=[
            pl.BlockSpec((ROW_TILE, D_MODEL), lambda i: (i, 0)),
            pl.BlockSpec((1, D_MODEL), lambda i: (0, 0)),
            pl.BlockSpec((1, 1, D_MODEL), lambda i: (i // steps_per_batch, 0, 0)),
            pl.BlockSpec((1, 1, D_MODEL), lambda i: (i // steps_per_batch, 0, 0)),
            pl.BlockSpec((D_MODEL, D_PROJ), lambda i: (0, 0),
                         pipeline_mode=pl.Buffered(1)),
        ],
        out_specs=pl.BlockSpec((ROW_TILE, D_PROJ), lambda i: (i, 0)),
        out_shape=jax.ShapeDtypeStruct((M, D_PROJ), BF16),
        compiler_params=pltpu.CompilerParams(
            dimension_semantics=("arbitrary",),
            vmem_limit_bytes=VMEM_LIMIT),
        name="norm_inproj",
    )(x2d, pre_g, shift, scale, w_bf16)


def _outproj_kernel(yg_ref, w_ref, x_ref, g_ref, gate_ref, o_ref):
    y = jnp.dot(yg_ref[...], w_ref[...], preferred_element_type=F32)
    ms = jnp.mean(y * y, axis=-1, keepdims=True)
    yn = y * lax.rsqrt(ms + RMS_EPS) * g_ref[...]
    o_ref[...] = x_ref[...] + gate_ref[0] * yn


def _out_projection(yg2d, w_bf16, x2d, post_g, gate, seq):
    M = x2d.shape[0]
    steps_per_batch = seq // ROW_TILE
    return pl.pallas_call(
        _outproj_kernel,
        grid=(M // ROW_TILE,),
        in_specs=[
            pl.BlockSpec((ROW_TILE, D_INNER), lambda i: (i, 0)),
            pl.BlockSpec((D_INNER, D_MODEL), lambda i: (0, 0),
                         pipeline_mode=pl.Buffered(1)),
            pl.BlockSpec((ROW_TILE, D_MODEL), lambda i: (i, 0)),
            pl.BlockSpec((1, D_MODEL), lambda i: (0, 0)),
            pl.BlockSpec((1, 1, D_MODEL), lambda i: (i // steps_per_batch, 0, 0)),
        ],
        out_specs=pl.BlockSpec((ROW_TILE, D_MODEL), lambda i: (i, 0)),
        out_shape=jax.ShapeDtypeStruct((M, D_MODEL), F32),
        input_output_aliases={2: 0},
        compiler_params=pltpu.CompilerParams(
            dimension_semantics=("arbitrary",),
            vmem_limit_bytes=VMEM_LIMIT),
        name="outproj_residual",
    )(yg2d, w_bf16, x2d, post_g, gate)


def _retention_tables(seq):
    H, dk, C = RET_HEADS, RET_QK, RET_CHUNK
    omega = 1.0 / (ROPE_BASE ** jnp.linspace(0.0, 1.0, dk // 2, dtype=F32))
    ang = jnp.arange(seq, dtype=F32)[:, None] * omega[None, :]
    cos, sin = jnp.cos(ang), jnp.sin(ang)
    log_gamma = jnp.log1p(-jnp.exp2(jnp.linspace(-5.0, -9.0, H, dtype=F32)))
    idx = jnp.arange(C, dtype=F32)
    diff = idx[:, None] - idx[None, :]
    decay_mat = jnp.where(
        diff[None] >= 0,
        jnp.exp(jnp.maximum(diff, 0.0)[None] * log_gamma[:, None, None]), 0.0)
    q_dec = jnp.exp((idx[None, :] + 1.0) * log_gamma[:, None])
    k_dec = jnp.exp((C - 1.0 - idx[None, :]) * log_gamma[:, None])
    chunk_dec = jnp.exp(C * log_gamma)
    decay_mat = decay_mat * dk ** -0.5
    k_dec = k_dec * dk ** -0.5
    q_dec = jnp.broadcast_to(q_dec[:, :, None], (H, C, dk))
    k_dec = jnp.broadcast_to(k_dec[:, :, None], (H, C, dk))
    chunk_dec = jnp.broadcast_to(chunk_dec[:, None, None], (H, 1, RET_V))
    return cos, sin, decay_mat, q_dec, k_dec, chunk_dec


def _rotate(t, cos, sin):
    half = RET_QK // 2
    t1, t2 = t[:, :half], t[:, half:]
    return jnp.concatenate([t1 * cos - t2 * sin, t2 * cos + t1 * sin], axis=-1)


def _retention_kernel(q_ref, k_ref, v_ref, g_ref, cos_ref, sin_ref, dm_ref,
                      qd_ref, kd_ref, cd_ref, gn_ref, o_ref, state_ref):
    @pl.when(pl.program_id(2) == 0)
    def _():
        state_ref[...] = jnp.zeros_like(state_ref)

    C = RET_CHUNK
    for c in range(RET_ROWS // C):
        rows = slice(c * C, (c + 1) * C)
        cos, sin = cos_ref[rows, :], sin_ref[rows, :]
        q = _rotate(q_ref[0, rows, :].astype(F32), cos, sin)
        k = _rotate(k_ref[0, rows, :].astype(F32), cos, sin)
        v = v_ref[0, rows, :]
        state = state_ref[...]
        scores = lax.dot_general(
            q.astype(BF16), k.astype(BF16), (((1,), (1,)), ((), ())),
            preferred_element_type=F32) * dm_ref[0]
        inner = jnp.dot(scores.astype(BF16), v, preferred_element_type=F32)
        cross = jnp.dot((q * qd_ref[0]).astype(BF16), state.astype(BF16),
                        preferred_element_type=F32)
        kv = lax.dot_general(
            (k * kd_ref[0]).astype(BF16), v, (((0,), (0,)), ((), ())),
            preferred_element_type=F32)
        state_ref[...] = state * cd_ref[0] + kv
        o = inner + cross
        mu = jnp.mean(o, axis=-1, keepdims=True)
        d = o - mu
        var = jnp.mean(d * d, axis=-1, keepdims=True)
        on = d * lax.rsqrt(var + GN_EPS) * gn_ref[...]
        o_ref[0, rows, :] = (on * _silu(g_ref[0, rows, :].astype(F32))).astype(BF16)


def _retention(proj3d, gn_g, tables):
    B, S, _ = proj3d.shape
    cos, sin, decay_mat, q_dec, k_dec, chunk_dec = tables
    H, dk, dv, C = RET_HEADS, RET_QK, RET_V, RET_CHUNK
    k_off = D_MODEL // dk
    v_off = 2 * D_MODEL // dv
    g_off = (2 * D_MODEL + D_INNER) // dv
    return pl.pallas_call(
        _retention_kernel,
        grid=(B, H, S // RET_ROWS),
        in_specs=[
            pl.BlockSpec((1, RET_ROWS, dk), lambda b, h, s: (b, s, h)),
            pl.BlockSpec((1, RET_ROWS, dk), lambda b, h, s: (b, s, k_off + h)),
            pl.BlockSpec((1, RET_ROWS, dv), lambda b, h, s: (b, s, v_off + h)),
            pl.BlockSpec((1, RET_ROWS, dv), lambda b, h, s: (b, s, g_off + h)),
            pl.BlockSpec((RET_ROWS, dk // 2), lambda b, h, s: (s, 0)),
            pl.BlockSpec((RET_ROWS, dk // 2), lambda b, h, s: (s, 0)),
            pl.BlockSpec((1, C, C), lambda b, h, s: (h, 0, 0)),
            pl.BlockSpec((1, C, dk), lambda b, h, s: (h, 0, 0)),
            pl.BlockSpec((1, C, dk), lambda b, h, s: (h, 0, 0)),
            pl.BlockSpec((1, 1, dv), lambda b, h, s: (h, 0, 0)),
            pl.BlockSpec((1, dv), lambda b, h, s: (0, h)),
        ],
        out_specs=pl.BlockSpec((1, RET_ROWS, dv), lambda b, h, s: (b, s, h)),
        out_shape=jax.ShapeDtypeStruct((B, S, D_INNER), BF16),
        scratch_shapes=[pltpu.VMEM((dk, dv), F32)],
        compiler_params=pltpu.CompilerParams(
            dimension_semantics=("arbitrary", "arbitrary", "arbitrary"),
            vmem_limit_bytes=VMEM_LIMIT),
        name="retention_mixer",
    )(proj3d, proj3d, proj3d, proj3d, cos, sin, decay_mat, q_dec, k_dec,
      chunk_dec, gn_g)


LOG2E = 1.4426950408889634


SB_EXIT_MASS = 152.0


def _sb_tile_terms(q, k_tile, tri, mask):
    z = lax.dot_general(q, k_tile, (((1,), (1,)), ((), ())),
                        preferred_element_type=F32)
    sp = jnp.maximum(z, 0.0) + jnp.log2(1.0 + jnp.exp2(-jnp.abs(z)))
    if mask is not None:
        sp = jnp.where(mask, sp, 0.0)
    from_s = jnp.dot(sp.astype(BF16), tri, preferred_element_type=F32)
    return z - from_s, from_s[:, 0:1]


def _sb_weights(base, r_prev, mask):
    a = jnp.exp2(base - r_prev)
    if mask is not None:
        a = jnp.where(mask, a, 0.0)
    return a.astype(BF16)


def _sb_kernel(q_ref, k_ref, v_ref, g_ref, o_ref, acc_ref, r_ref):
    qi = pl.program_id(2)
    lane = lax.broadcasted_iota(jnp.int32, (SB_TQ, LANES), 1)
    qf = q_ref[0].astype(F32) * (SB_QK ** -0.5 * LOG2E)
    q_heads = [jnp.where(lane < SB_QK, qf, 0.0).astype(BF16),
               jnp.where(lane >= SB_QK, qf, 0.0).astype(BF16)]
    row = lax.broadcasted_iota(jnp.int32, (SB_TK, SB_TK), 0)
    col = lax.broadcasted_iota(jnp.int32, (SB_TK, SB_TK), 1)
    tri = jnp.where(row >= col, 1.0, 0.0).astype(BF16)
    diag_mask = col < row

    def add_tiles(kb_near, n, near_mask, first):
        kbs = [kb_near - t for t in range(n)]
        starts = [pl.multiple_of(jnp.maximum(kb, 0) * SB_TK, SB_TK) for kb in kbs]
        k_tiles = [k_ref[0, pl.ds(s, SB_TK), :] for s in starts]
        v_tiles = [v_ref[0, pl.ds(s, SB_TK), :] for s in starts]
        r_min = []
        for h in range(2):
            vcols = slice(h * SB_V, (h + 1) * SB_V)
            terms = [_sb_tile_terms(q_heads[h], k_tiles[t], tri,
                                    near_mask if t == 0 else None)
                     for t in range(n)]
            r = jnp.zeros((SB_TQ, 1), F32) if first else r_ref[h]
            contrib = None
            for t in range(n):
                base, row_sum = terms[t]
                a = _sb_weights(base, r, near_mask if t == 0 else None)
                c = jnp.dot(a, v_tiles[t][:, vcols], preferred_element_type=F32)
                if t == 0:
                    contrib = c
                    r = r + row_sum
                else:
                    contrib = contrib + jnp.where(kbs[t] >= 0, c, 0.0)
                    r = r + jnp.where(kbs[t] >= 0, row_sum, 0.0)
            if first:
                acc_ref[h] = contrib
            else:
                acc_ref[h] += contrib
            r_ref[h] = r
            r_min.append(jnp.min(r))
        return jnp.minimum(r_min[0], r_min[1])

    mass0 = add_tiles(qi, SB_FIRST_TILES, diag_mask, True)
    n_more = (qi + 1 - SB_FIRST_TILES + SB_LOOP_TILES - 1) // SB_LOOP_TILES

    def more_tiles(carry):
        j, mass = carry
        return jnp.logical_and(j < n_more, mass < SB_EXIT_MASS)

    def body(carry):
        j, _ = carry
        kb = qi - SB_FIRST_TILES - SB_LOOP_TILES * j
        return j + 1, add_tiles(kb, SB_LOOP_TILES, None, False)

    lax.while_loop(more_tiles, body, (jnp.int32(0), mass0))

    for h in range(2):
        cols = slice(h * SB_V, (h + 1) * SB_V)
        gate = _silu(g_ref[0, :, cols].astype(F32))
        o_ref[0, :, cols] = (acc_ref[h] * gate).astype(BF16)


def _stick_breaking(proj3d):
    B, S, _ = proj3d.shape
    pair_v = 2 * SB_V
    k_off = D_MODEL // LANES
    v_off = 2 * D_MODEL // pair_v
    g_off = (2 * D_MODEL + D_INNER) // pair_v
    return pl.pallas_call(
        _sb_kernel,
        grid=(B, SB_HEADS // 2, S // SB_TQ),
        in_specs=[
            pl.BlockSpec((1, SB_TQ, LANES), lambda b, p, i: (b, i, p)),
            pl.BlockSpec((1, S, LANES), lambda b, p, i: (b, 0, k_off + p)),
            pl.BlockSpec((1, S, pair_v), lambda b, p, i: (b, 0, v_off + p)),
            pl.BlockSpec((1, SB_TQ, pair_v), lambda b, p, i: (b, i, g_off + p)),
        ],
        out_specs=pl.BlockSpec((1, SB_TQ, pair_v), lambda b, p, i: (b, i, p)),
        out_shape=jax.ShapeDtypeStruct((B, S, D_INNER), BF16),
        scratch_shapes=[pltpu.VMEM((2, SB_TQ, SB_V), F32),
                        pltpu.VMEM((2, SB_TQ, 1), F32)],
        compiler_params=pltpu.CompilerParams(
            dimension_semantics=("arbitrary", "arbitrary", "arbitrary"),
            vmem_limit_bytes=VMEM_LIMIT),
        name="stick_breaking_mixer",
    )(proj3d, proj3d, proj3d, proj3d)


def kernel(x, c, w_in, w_out, w_mod, b_mod, pre_norm, post_norm, ret_gn):
    B, S, D = x.shape
    assert D == D_MODEL and S % ROW_TILE == 0 and S % RET_ROWS == 0
    assert S % SB_TQ == 0 and SB_TQ == SB_TK
    mod = _modulation(c, w_mod, b_mod)
    w_in_b = w_in.astype(BF16)
    w_out_b = w_out.astype(BF16)
    tables = _retention_tables(S)
    x2d = x.reshape(B * S, D)
    for i in range(DEPTH):
        shift = mod[i, :, :D].reshape(B, 1, D)
        scale = mod[i, :, D:2 * D].reshape(B, 1, D)
        gate = mod[i, :, 2 * D:].reshape(B, 1, D)
        proj = _in_projection(x2d, pre_norm[i].reshape(1, D), shift, scale,
                              w_in_b[i], S)
        proj3d = proj.reshape(B, S, D_PROJ)
        if i % 2 == 0:
            yg = _retention(proj3d, ret_gn[i // 2].reshape(1, D_INNER), tables)
        else:
            yg = _stick_breaking(proj3d)
        x2d = _out_projection(yg.reshape(B * S, D_INNER), w_out_b[i], x2d,
                              post_norm[i].reshape(1, D), gate, S)
    return x2d.reshape(B, S, D)
```

```python
import functools

import jax
import jax.numpy as jnp
from jax import lax
from jax.experimental import pallas as pl
from jax.experimental.pallas import tpu as pltpu

D_MODEL = 1024
DEPTH = 4
D_INNER = 2 * D_MODEL
RET_HEADS = 4
RET_QK = D_MODEL // RET_HEADS
RET_V = D_INNER // RET_HEADS
RET_CHUNK = 256
SB_HEADS = 16
SB_QK = D_MODEL // SB_HEADS
SB_V = D_INNER // SB_HEADS
D_PROJ = 2 * D_MODEL + 2 * D_INNER
RMS_EPS = 1e-6
GN_EPS = 1e-5
ROPE_BASE = 10000.0

LANES = 128
VMEM_LIMIT = 56 * 1024 * 1024

ROW_TILE = 512
N_TILE = 512
RET_ROWS = 1024
SB_TQ = 256
SB_TK = 256
SB_SUBS = 2
SB_FIRST_TILES = 3
SB_LOOP_TILES = 3

BF16 = jnp.bfloat16
F32 = jnp.float32


def _silu(v):
    return v / (1.0 + jnp.exp(-v))


def _split_bf16(v):
    hi = v.astype(BF16)
    lo = (v - hi.astype(F32)).astype(BF16)
    return hi, lo


def _mod_kernel(c_ref, w_ref, b_ref, o_ref):
    a_hi, a_lo = _split_bf16(_silu(c_ref[...]))
    w_hi, w_lo = _split_bf16(w_ref[0])
    acc = jnp.dot(a_hi, w_hi, preferred_element_type=F32)
    acc += jnp.dot(a_hi, w_lo, preferred_element_type=F32)
    acc += jnp.dot(a_lo, w_hi, preferred_element_type=F32)
    o_ref[0] = acc + b_ref[0]


def _modulation(c, w_mod, b_mod):
    B = c.shape[0]
    return pl.pallas_call(
        _mod_kernel,
        grid=(DEPTH, 3),
        in_specs=[
            pl.BlockSpec((B, D_MODEL), lambda i, j: (0, 0)),
            pl.BlockSpec((1, D_MODEL, D_MODEL), lambda i, j: (i, 0, j)),
            pl.BlockSpec((1, 1, D_MODEL), lambda i, j: (i, 0, j)),
        ],
        out_specs=pl.BlockSpec((1, B, D_MODEL), lambda i, j: (i, 0, j)),
        out_shape=jax.ShapeDtypeStruct((DEPTH, B, 3 * D_MODEL), F32),
        compiler_params=pltpu.CompilerParams(
            dimension_semantics=("arbitrary", "arbitrary"),
            vmem_limit_bytes=VMEM_LIMIT),
        name="adaln_modulation",
    )(c, w_mod, b_mod.reshape(DEPTH, 1, 3 * D_MODEL))


def _inproj_kernel(x_ref, g_ref, shift_ref, scale_ref, w_ref, o_ref):
    xf = x_ref[...]
    ms = jnp.mean(xf * xf, axis=-1, keepdims=True)
    y = xf * lax.rsqrt(ms + RMS_EPS) * g_ref[...]
    h = (y * (1.0 + scale_ref[0]) + shift_ref[0]).astype(BF16)
    for n in range(D_PROJ // N_TILE):
        cols = slice(n * N_TILE, (n + 1) * N_TILE)
        o_ref[:, cols] = jnp.dot(
            h, w_ref[:, cols], preferred_element_type=F32).astype(BF16)


def _in_projection(x2d, pre_g, shift, scale, w_bf16, seq):
    M = x2d.shape[0]
    steps_per_batch = seq // ROW_TILE
    return pl.pallas_call(
        _inproj_kernel,
        grid=(M // ROW_TILE,),
        in_specs=[
            pl.BlockSpec((ROW_TILE, D_MODEL), lambda i: (i, 0)),
            pl.BlockSpec((1, D_MODEL), lambda i: (0, 0)),
            pl.BlockSpec((1, 1, D_MODEL), lambda i: (i // steps_per_batch, 0, 0)),
            pl.BlockSpec((1, 1, D_MODEL), lambda i: (i // steps_per_batch, 0, 0)),
            pl.BlockSpec((D_MODEL, D_PROJ), lambda i: (0, 0),
                         pipeline_mode=pl.Buffered(1)),
        ],
        out_specs=pl.BlockSpec((ROW_TILE, D_PROJ), lambda i: (i, 0)),
        out_shape=jax.ShapeDtypeStruct((M, D_PROJ), BF16),
        compiler_params=pltpu.CompilerParams(
            dimension_semantics=("arbitrary",),
            vmem_limit_bytes=VMEM_LIMIT),
        name="norm_inproj",
    )(x2d, pre_g, shift, scale, w_bf16)


def _outproj_kernel(yg_ref, w_ref, x_ref, g_ref, gate_ref, o_ref):
    y = jnp.dot(yg_ref[...], w_ref[...], preferred_element_type=F32)
    ms = jnp.mean(y * y, axis=-1, keepdims=True)
    yn = y * lax.rsqrt(ms + RMS_EPS) * g_ref[...]
    o_ref[...] = x_ref[...] + gate_ref[0] * yn


def _out_projection(yg2d, w_bf16, x2d, post_g, gate, seq, in_place):
    M = x2d.shape[0]
    steps_per_batch = seq // ROW_TILE
    return pl.pallas_call(
        _outproj_kernel,
        grid=(M // ROW_TILE,),
        in_specs=[
            pl.BlockSpec((ROW_TILE, D_INNER), lambda i: (i, 0)),
            pl.BlockSpec((D_INNER, D_MODEL), lambda i: (0, 0),
                         pipeline_mode=pl.Buffered(1)),
            pl.BlockSpec((ROW_TILE, D_MODEL), lambda i: (i, 0)),
            pl.BlockSpec((1, D_MODEL), lambda i: (0, 0)),
            pl.BlockSpec((1, 1, D_MODEL), lambda i: (i // steps_per_batch, 0, 0)),
        ],
        out_specs=pl.BlockSpec((ROW_TILE, D_MODEL), lambda i: (i, 0)),
        out_shape=jax.ShapeDtypeStruct((M, D_MODEL), F32),
        input_output_aliases={2: 0} if in_place else {},
        compiler_params=pltpu.CompilerParams(
            dimension_semantics=("arbitrary",),
            vmem_limit_bytes=VMEM_LIMIT),
        name="outproj_residual",
    )(yg2d, w_bf16, x2d, post_g, gate)


def _retention_tables(seq):
    H, dk, C = RET_HEADS, RET_QK, RET_CHUNK
    omega = 1.0 / (ROPE_BASE ** jnp.linspace(0.0, 1.0, dk // 2, dtype=F32))
    ang = jnp.arange(seq, dtype=F32)[:, None] * omega[None, :]
    cos, sin = jnp.cos(ang), jnp.sin(ang)
    log_gamma = jnp.log1p(-jnp.exp2(jnp.linspace(-5.0, -9.0, H, dtype=F32)))
    idx = jnp.arange(C, dtype=F32)
    diff = idx[:, None] - idx[None, :]
    decay_mat = jnp.where(
        diff[None] >= 0,
        jnp.exp(jnp.maximum(diff, 0.0)[None] * log_gamma[:, None, None]), 0.0)
    q_dec = jnp.exp((idx[None, :] + 1.0) * log_gamma[:, None])
    k_dec = jnp.exp((C - 1.0 - idx[None, :]) * log_gamma[:, None])
    chunk_dec = jnp.exp(C * log_gamma)
    decay_mat = decay_mat * dk ** -0.5
    k_dec = k_dec * dk ** -0.5
    q_dec = jnp.broadcast_to(q_dec[:, :, None], (H, C, dk))
    k_dec = jnp.broadcast_to(k_dec[:, :, None], (H, C, dk))
    chunk_dec = jnp.broadcast_to(chunk_dec[:, None, None], (H, 1, RET_V))
    return cos, sin, decay_mat, q_dec, k_dec, chunk_dec


def _rotate(t, cos, sin):
    half = RET_QK // 2
    t1, t2 = t[:, :half], t[:, half:]
    return jnp.concatenate([t1 * cos - t2 * sin, t2 * cos + t1 * sin], axis=-1)


def _retention_kernel(q_ref, k_ref, v_ref, g_ref, cos_ref, sin_ref, dm_ref,
                      qd_ref, kd_ref, cd_ref, gn_ref, o_ref, state_ref):
    @pl.when(pl.program_id(2) == 0)
    def _():
        state_ref[...] = jnp.zeros_like(state_ref)

    C = RET_CHUNK
    for c in range(RET_ROWS // C):
        rows = slice(c * C, (c + 1) * C)
        cos, sin = cos_ref[rows, :], sin_ref[rows, :]
        q = _rotate(q_ref[0, rows, :].astype(F32), cos, sin)
        k = _rotate(k_ref[0, rows, :].astype(F32), cos, sin)
        v = v_ref[0, rows, :]
        state = state_ref[...]
        scores = lax.dot_general(
            q.astype(BF16), k.astype(BF16), (((1,), (1,)), ((), ())),
            preferred_element_type=F32) * dm_ref[0]
        inner = jnp.dot(scores.astype(BF16), v, preferred_element_type=F32)
        cross = jnp.dot((q * qd_ref[0]).astype(BF16), state.astype(BF16),
                        preferred_element_type=F32)
        kv = lax.dot_general(
            (k * kd_ref[0]).astype(BF16), v, (((0,), (0,)), ((), ())),
            preferred_element_type=F32)
        state_ref[...] = state * cd_ref[0] + kv
        o = inner + cross
        mu = jnp.mean(o, axis=-1, keepdims=True)
        d = o - mu
        var = jnp.mean(d * d, axis=-1, keepdims=True)
        on = d * lax.rsqrt(var + GN_EPS) * gn_ref[...]
        o_ref[0, rows, :] = (on * _silu(g_ref[0, rows, :].astype(F32))).astype(BF16)


def _retention(proj3d, gn_g, tables):
    B, S, _ = proj3d.shape
    cos, sin, decay_mat, q_dec, k_dec, chunk_dec = tables
    H, dk, dv, C = RET_HEADS, RET_QK, RET_V, RET_CHUNK
    k_off = D_MODEL // dk
    v_off = 2 * D_MODEL // dv
    g_off = (2 * D_MODEL + D_INNER) // dv
    return pl.pallas_call(
        _retention_kernel,
        grid=(B, H, S // RET_ROWS),
        in_specs=[
            pl.BlockSpec((1, RET_ROWS, dk), lambda b, h, s: (b, s, h)),
            pl.BlockSpec((1, RET_ROWS, dk), lambda b, h, s: (b, s, k_off + h)),
            pl.BlockSpec((1, RET_ROWS, dv), lambda b, h, s: (b, s, v_off + h)),
            pl.BlockSpec((1, RET_ROWS, dv), lambda b, h, s: (b, s, g_off + h)),
            pl.BlockSpec((RET_ROWS, dk // 2), lambda b, h, s: (s, 0)),
            pl.BlockSpec((RET_ROWS, dk // 2), lambda b, h, s: (s, 0)),
            pl.BlockSpec((1, C, C), lambda b, h, s: (h, 0, 0)),
            pl.BlockSpec((1, C, dk), lambda b, h, s: (h, 0, 0)),
            pl.BlockSpec((1, C, dk), lambda b, h, s: (h, 0, 0)),
            pl.BlockSpec((1, 1, dv), lambda b, h, s: (h, 0, 0)),
            pl.BlockSpec((1, dv), lambda b, h, s: (0, h)),
        ],
        out_specs=pl.BlockSpec((1, RET_ROWS, dv), lambda b, h, s: (b, s, h)),
        out_shape=jax.ShapeDtypeStruct((B, S, D_INNER), BF16),
        scratch_shapes=[pltpu.VMEM((dk, dv), F32)],
        compiler_params=pltpu.CompilerParams(
            dimension_semantics=("arbitrary", "arbitrary", "arbitrary"),
            vmem_limit_bytes=VMEM_LIMIT),
        name="retention_mixer",
    )(proj3d, proj3d, proj3d, proj3d, cos, sin, decay_mat, q_dec, k_dec,
      chunk_dec, gn_g)


LOG2E = 1.4426950408889634


SB_EXIT_MASS = 152.0
SB_Z_CAP = 64.0


def _sb_tile_terms(q, k_tile, tri, mask):
    z = lax.dot_general(q, k_tile, (((1,), (1,)), ((), ())),
                        preferred_element_type=F32)
    sp = jnp.maximum(z, jnp.log2(1.0 + jnp.exp2(jnp.minimum(z, SB_Z_CAP))))
    if mask is not None:
        sp = jnp.where(mask, sp, 0.0)
    after = jnp.dot(sp.astype(BF16), tri, preferred_element_type=F32)
    return (z - sp) - after, after[:, 0:1] + sp[:, 0:1]


def _sb_weights(base, r_prev, mask):
    a = jnp.exp2(base - r_prev)
    if mask is not None:
        a = jnp.where(mask, a, 0.0)
    return a.astype(BF16)


def _sb_kernel(q_ref, k_ref, v_ref, g_ref, o_ref, acc_ref, r_ref):
    step = pl.program_id(2)
    lane = lax.broadcasted_iota(jnp.int32, (SB_TQ, LANES), 1)
    row = lax.broadcasted_iota(jnp.int32, (SB_TK, SB_TK), 0)
    col = lax.broadcasted_iota(jnp.int32, (SB_TK, SB_TK), 1)
    tri = jnp.where(row > col, 1.0, 0.0).astype(BF16)
    diag_mask = col < row

    def q_heads(sub):
        qf = q_ref[0, sub * SB_TQ:(sub + 1) * SB_TQ, :].astype(F32)
        qf = qf * (SB_QK ** -0.5 * LOG2E)
        return [jnp.where(lane < SB_QK, qf, 0.0).astype(BF16),
                jnp.where(lane >= SB_QK, qf, 0.0).astype(BF16)]

    def add_tiles(sub, kb_near, n, near_mask, first):
        q = q_heads(sub)
        kbs = [kb_near - t for t in range(n)]
        starts = [pl.multiple_of(jnp.maximum(kb, 0) * SB_TK, SB_TK) for kb in kbs]
        k_tiles = [k_ref[0, pl.ds(s, SB_TK), :] for s in starts]
        v_tiles = [v_ref[0, pl.ds(s, SB_TK), :] for s in starts]
        r_min = []
        for h in range(2):
            vcols = slice(h * SB_V, (h + 1) * SB_V)
            terms = [_sb_tile_terms(q[h], k_tiles[t], tri,
                                    near_mask if t == 0 else None)
                     for t in range(n)]
            r = jnp.zeros((SB_TQ, 1), F32) if first else r_ref[sub, h]
            contrib = None
            for t in range(n):
                base, row_sum = terms[t]
                a = _sb_weights(base, r, near_mask if t == 0 else None)
                c = jnp.dot(a, v_tiles[t][:, vcols], preferred_element_type=F32)
                if t == 0:
                    contrib = c
                    r = r + row_sum
                else:
                    contrib = contrib + jnp.where(kbs[t] >= 0, c, 0.0)
                    r = r + jnp.where(kbs[t] >= 0, row_sum, 0.0)
            if first:
                acc_ref[sub, h] = contrib
            else:
                acc_ref[sub, h] += contrib
            r_ref[sub, h] = r
            r_min.append(jnp.min(r))
        return jnp.minimum(r_min[0], r_min[1])

    blocks = [step * SB_SUBS + sub for sub in range(SB_SUBS)]
    masses = tuple(add_tiles(sub, blocks[sub], SB_FIRST_TILES, diag_mask, True)
                   for sub in range(SB_SUBS))
    n_more = [(qb + 1 - SB_FIRST_TILES + SB_LOOP_TILES - 1) // SB_LOOP_TILES
              for qb in blocks]

    def unfinished(j, mass, sub):
        return jnp.logical_and(j < n_more[sub], mass < SB_EXIT_MASS)

    def more_tiles(carry):
        j, masses = carry
        todo = [unfinished(j, masses[sub], sub) for sub in range(SB_SUBS)]
        return functools.reduce(jnp.logical_or, todo)

    def body(carry):
        j, masses = carry
        new = []
        for sub in range(SB_SUBS):
            kb = blocks[sub] - SB_FIRST_TILES - SB_LOOP_TILES * j
            new.append(lax.cond(
                unfinished(j, masses[sub], sub),
                functools.partial(add_tiles, sub, kb, SB_LOOP_TILES, None, False),
                functools.partial(lambda m: m, masses[sub])))
        return j + 1, tuple(new)

    lax.while_loop(more_tiles, body, (jnp.int32(0), masses))

    for sub in range(SB_SUBS):
        rows = slice(sub * SB_TQ, (sub + 1) * SB_TQ)
        for h in range(2):
            cols = slice(h * SB_V, (h + 1) * SB_V)
            gate = _silu(g_ref[0, rows, cols].astype(F32))
            o_ref[0, rows, cols] = (acc_ref[sub, h] * gate).astype(BF16)


def _stick_breaking(proj3d):
    B, S, _ = proj3d.shape
    pair_v = 2 * SB_V
    step_rows = SB_SUBS * SB_TQ
    k_off = D_MODEL // LANES
    v_off = 2 * D_MODEL // pair_v
    g_off = (2 * D_MODEL + D_INNER) // pair_v
    return pl.pallas_call(
        _sb_kernel,
        grid=(B, SB_HEADS // 2, S // step_rows),
        in_specs=[
            pl.BlockSpec((1, step_rows, LANES), lambda b, p, i: (b, i, p)),
            pl.BlockSpec((1, S, LANES), lambda b, p, i: (b, 0, k_off + p)),
            pl.BlockSpec((1, S, pair_v), lambda b, p, i: (b, 0, v_off + p)),
            pl.BlockSpec((1, step_rows, pair_v), lambda b, p, i: (b, i, g_off + p)),
        ],
        out_specs=pl.BlockSpec((1, step_rows, pair_v), lambda b, p, i: (b, i, p)),
        out_shape=jax.ShapeDtypeStruct((B, S, D_INNER), BF16),
        scratch_shapes=[pltpu.VMEM((SB_SUBS, 2, SB_TQ, SB_V), F32),
                        pltpu.VMEM((SB_SUBS, 2, SB_TQ, 1), F32)],
        compiler_params=pltpu.CompilerParams(
            dimension_semantics=("arbitrary", "arbitrary", "arbitrary"),
            vmem_limit_bytes=VMEM_LIMIT),
        name="stick_breaking_mixer",
    )(proj3d, proj3d, proj3d, proj3d)


def kernel(x, c, w_in, w_out, w_mod, b_mod, pre_norm, post_norm, ret_gn):
    B, S, D = x.shape
    assert D == D_MODEL and S % ROW_TILE == 0 and S % RET_ROWS == 0
    assert S % (SB_SUBS * SB_TQ) == 0 and SB_TQ == SB_TK
    mod = _modulation(c, w_mod, b_mod)
    w_in_b = w_in.astype(BF16)
    w_out_b = w_out.astype(BF16)
    tables = _retention_tables(S)
    x2d = x.reshape(B * S, D)
    for i in range(DEPTH):
        shift = mod[i, :, :D].reshape(B, 1, D)
        scale = mod[i, :, D:2 * D].reshape(B, 1, D)
        gate = mod[i, :, 2 * D:].reshape(B, 1, D)
        proj = _in_projection(x2d, pre_norm[i].reshape(1, D), shift, scale,
                              w_in_b[i], S)
        proj3d = proj.reshape(B, S, D_PROJ)
        if i % 2 == 0:
            yg = _retention(proj3d, ret_gn[i // 2].reshape(1, D_INNER), tables)
        else:
            yg = _stick_breaking(proj3d)
        x2d = _out_projection(yg.reshape(B * S, D_INNER), w_out_b[i], x2d,
                              post_norm[i].reshape(1, D), gate, S, in_place=i > 0)
    return x2d.reshape(B, S, D)
```

```python
import functools

import jax
import jax.numpy as jnp
from jax import lax
from jax.experimental import pallas as pl
from jax.experimental.pallas import tpu as pltpu

D_MODEL = 1024
DEPTH = 4
D_INNER = 2 * D_MODEL
RET_HEADS = 4
RET_QK = D_MODEL // RET_HEADS
RET_V = D_INNER // RET_HEADS
RET_CHUNK = 256
SB_HEADS = 16
SB_QK = D_MODEL // SB_HEADS
SB_V = D_INNER // SB_HEADS
D_PROJ = 2 * D_MODEL + 2 * D_INNER
RMS_EPS = 1e-6
GN_EPS = 1e-5
ROPE_BASE = 10000.0

LANES = 128
VMEM_LIMIT = 56 * 1024 * 1024

ROW_TILE = 512
OUT_ROW_TILE = 1024
N_TILE = 512
RET_ROWS = 1024
SB_TQ = 256
SB_TK = 256
SB_SUBS = 4
SB_FIRST_TILES = 3
SB_LOOP_TILES = 2

BF16 = jnp.bfloat16
F32 = jnp.float32


def _silu(v):
    return v / (1.0 + jnp.exp(-v))


def _split_bf16(v):
    hi = v.astype(BF16)
    lo = (v - hi.astype(F32)).astype(BF16)
    return hi, lo


def _mod_kernel(c_ref, w_ref, b_ref, o_ref):
    a_hi, a_lo = _split_bf16(_silu(c_ref[...]))
    w_hi, w_lo = _split_bf16(w_ref[0])
    acc = jnp.dot(a_hi, w_hi, preferred_element_type=F32)
    acc += jnp.dot(a_hi, w_lo, preferred_element_type=F32)
    acc += jnp.dot(a_lo, w_hi, preferred_element_type=F32)
    o_ref[0] = acc + b_ref[0]


def _modulation(c, w_mod, b_mod):
    B = c.shape[0]
    return pl.pallas_call(
        _mod_kernel,
        grid=(DEPTH, 3),
        in_specs=[
            pl.BlockSpec((B, D_MODEL), lambda i, j: (0, 0)),
            pl.BlockSpec((1, D_MODEL, D_MODEL), lambda i, j: (i, 0, j)),
            pl.BlockSpec((1, 1, D_MODEL), lambda i, j: (i, 0, j)),
        ],
        out_specs=pl.BlockSpec((1, B, D_MODEL), lambda i, j: (i, 0, j)),
        out_shape=jax.ShapeDtypeStruct((DEPTH, B, 3 * D_MODEL), F32),
        compiler_params=pltpu.CompilerParams(
            dimension_semantics=("arbitrary", "arbitrary"),
            vmem_limit_bytes=VMEM_LIMIT),
        name="adaln_modulation",
    )(c, w_mod, b_mod.reshape(DEPTH, 1, 3 * D_MODEL))


def _inproj_kernel(x_ref, g_ref, shift_ref, scale_ref, w_ref, o_ref):
    xf = x_ref[...]
    ms = jnp.mean(xf * xf, axis=-1, keepdims=True)
    y = xf * lax.rsqrt(ms + RMS_EPS) * g_ref[...]
    h = (y * (1.0 + scale_ref[0]) + shift_ref[0]).astype(BF16)
    for n in range(D_PROJ // N_TILE):
        cols = slice(n * N_TILE, (n + 1) * N_TILE)
        o_ref[:, cols] = jnp.dot(
            h, w_ref[:, cols], preferred_element_type=F32).astype(BF16)


def _in_projection(x2d, pre_g, shift, scale, w_bf16, seq):
    M = x2d.shape[0]
    steps_per_batch = seq // ROW_TILE
    return pl.pallas_call(
        _inproj_kernel,
        grid=(M // ROW_TILE,),
        in_specs=[
            pl.BlockSpec((ROW_TILE, D_MODEL), lambda i: (i, 0)),
            pl.BlockSpec((1, D_MODEL), lambda i: (0, 0)),
            pl.BlockSpec((1, 1, D_MODEL), lambda i: (i // steps_per_batch, 0, 0)),
            pl.BlockSpec((1, 1, D_MODEL), lambda i: (i // steps_per_batch, 0, 0)),
            pl.BlockSpec((D_MODEL, D_PROJ), lambda i: (0, 0),
                         pipeline_mode=pl.Buffered(1)),
        ],
        out_specs=pl.BlockSpec((ROW_TILE, D_PROJ), lambda i: (i, 0)),
        out_shape=jax.ShapeDtypeStruct((M, D_PROJ), BF16),
        compiler_params=pltpu.CompilerParams(
            dimension_semantics=("arbitrary",),
            vmem_limit_bytes=VMEM_LIMIT),
        name="norm_inproj",
    )(x2d, pre_g, shift, scale, w_bf16)


def _outproj_kernel(yg_ref, w_ref, x_ref, g_ref, gate_ref, o_ref):
    y = jnp.dot(yg_ref[...], w_ref[...], preferred_element_type=F32)
    ms = jnp.mean(y * y, axis=-1, keepdims=True)
    yn = y * lax.rsqrt(ms + RMS_EPS) * g_ref[...]
    o_ref[...] = x_ref[...] + gate_ref[0] * yn


def _out_projection(yg2d, w_bf16, x2d, post_g, gate, seq, in_place):
    M = x2d.shape[0]
    steps_per_batch = seq // OUT_ROW_TILE
    return pl.pallas_call(
        _outproj_kernel,
        grid=(M // OUT_ROW_TILE,),
        in_specs=[
            pl.BlockSpec((OUT_ROW_TILE, D_INNER), lambda i: (i, 0)),
            pl.BlockSpec((D_INNER, D_MODEL), lambda i: (0, 0),
                         pipeline_mode=pl.Buffered(1)),
            pl.BlockSpec((OUT_ROW_TILE, D_MODEL), lambda i: (i, 0)),
            pl.BlockSpec((1, D_MODEL), lambda i: (0, 0)),
            pl.BlockSpec((1, 1, D_MODEL), lambda i: (i // steps_per_batch, 0, 0)),
        ],
        out_specs=pl.BlockSpec((OUT_ROW_TILE, D_MODEL), lambda i: (i, 0)),
        out_shape=jax.ShapeDtypeStruct((M, D_MODEL), F32),
        input_output_aliases={2: 0} if in_place else {},
        compiler_params=pltpu.CompilerParams(
            dimension_semantics=("arbitrary",),
            vmem_limit_bytes=VMEM_LIMIT),
        name="outproj_residual",
    )(yg2d, w_bf16, x2d, post_g, gate)


def _retention_tables(seq):
    H, dk, C = RET_HEADS, RET_QK, RET_CHUNK
    omega = 1.0 / (ROPE_BASE ** jnp.linspace(0.0, 1.0, dk // 2, dtype=F32))
    ang = jnp.arange(seq, dtype=F32)[:, None] * omega[None, :]
    cos, sin = jnp.cos(ang), jnp.sin(ang)
    log_gamma = jnp.log1p(-jnp.exp2(jnp.linspace(-5.0, -9.0, H, dtype=F32)))
    idx = jnp.arange(C, dtype=F32)
    diff = idx[:, None] - idx[None, :]
    decay_mat = jnp.where(
        diff[None] >= 0,
        jnp.exp(jnp.maximum(diff, 0.0)[None] * log_gamma[:, None, None]), 0.0)
    q_dec = jnp.exp((idx[None, :] + 1.0) * log_gamma[:, None])
    k_dec = jnp.exp((C - 1.0 - idx[None, :]) * log_gamma[:, None])
    chunk_dec = jnp.exp(C * log_gamma)
    decay_mat = decay_mat * dk ** -0.5
    k_dec = k_dec * dk ** -0.5
    q_dec = jnp.broadcast_to(q_dec[:, :, None], (H, C, dk))
    k_dec = jnp.broadcast_to(k_dec[:, :, None], (H, C, dk))
    chunk_dec = jnp.broadcast_to(chunk_dec[:, None, None], (H, 1, RET_V))
    return cos, sin, decay_mat, q_dec, k_dec, chunk_dec


def _rotate(t, cos, sin):
    half = RET_QK // 2
    t1, t2 = t[:, :half], t[:, half:]
    return jnp.concatenate([t1 * cos - t2 * sin, t2 * cos + t1 * sin], axis=-1)


def _retention_kernel(q_ref, k_ref, v_ref, g_ref, cos_ref, sin_ref, dm_ref,
                      qd_ref, kd_ref, cd_ref, gn_ref, o_ref, state_ref):
    @pl.when(pl.program_id(2) == 0)
    def _():
        state_ref[...] = jnp.zeros_like(state_ref)

    C = RET_CHUNK
    for c in range(RET_ROWS // C):
        rows = slice(c * C, (c + 1) * C)
        cos, sin = cos_ref[rows, :], sin_ref[rows, :]
        q = _rotate(q_ref[0, rows, :].astype(F32), cos, sin)
        k = _rotate(k_ref[0, rows, :].astype(F32), cos, sin)
        v = v_ref[0, rows, :]
        state = state_ref[...]
        scores = lax.dot_general(
            q.astype(BF16), k.astype(BF16), (((1,), (1,)), ((), ())),
            preferred_element_type=F32) * dm_ref[0]
        inner = jnp.dot(scores.astype(BF16), v, preferred_element_type=F32)
        cross = jnp.dot((q * qd_ref[0]).astype(BF16), state.astype(BF16),
                        preferred_element_type=F32)
        kv = lax.dot_general(
            (k * kd_ref[0]).astype(BF16), v, (((0,), (0,)), ((), ())),
            preferred_element_type=F32)
        state_ref[...] = state * cd_ref[0] + kv
        o = inner + cross
        mu = jnp.mean(o, axis=-1, keepdims=True)
        d = o - mu
        var = jnp.mean(d * d, axis=-1, keepdims=True)
        on = d * lax.rsqrt(var + GN_EPS) * gn_ref[...]
        o_ref[0, rows, :] = (on * _silu(g_ref[0, rows, :].astype(F32))).astype(BF16)


def _retention(proj3d, gn_g, tables):
    B, S, _ = proj3d.shape
    cos, sin, decay_mat, q_dec, k_dec, chunk_dec = tables
    H, dk, dv, C = RET_HEADS, RET_QK, RET_V, RET_CHUNK
    k_off = D_MODEL // dk
    v_off = 2 * D_MODEL // dv
    g_off = (2 * D_MODEL + D_INNER) // dv
    return pl.pallas_call(
        _retention_kernel,
        grid=(B, H, S // RET_ROWS),
        in_specs=[
            pl.BlockSpec((1, RET_ROWS, dk), lambda b, h, s: (b, s, h)),
            pl.BlockSpec((1, RET_ROWS, dk), lambda b, h, s: (b, s, k_off + h)),
            pl.BlockSpec((1, RET_ROWS, dv), lambda b, h, s: (b, s, v_off + h)),
            pl.BlockSpec((1, RET_ROWS, dv), lambda b, h, s: (b, s, g_off + h)),
            pl.BlockSpec((RET_ROWS, dk // 2), lambda b, h, s: (s, 0)),
            pl.BlockSpec((RET_ROWS, dk // 2), lambda b, h, s: (s, 0)),
            pl.BlockSpec((1, C, C), lambda b, h, s: (h, 0, 0)),
            pl.BlockSpec((1, C, dk), lambda b, h, s: (h, 0, 0)),
            pl.BlockSpec((1, C, dk), lambda b, h, s: (h, 0, 0)),
            pl.BlockSpec((1, 1, dv), lambda b, h, s: (h, 0, 0)),
            pl.BlockSpec((1, dv), lambda b, h, s: (0, h)),
        ],
        out_specs=pl.BlockSpec((1, RET_ROWS, dv), lambda b, h, s: (b, s, h)),
        out_shape=jax.ShapeDtypeStruct((B, S, D_INNER), BF16),
        scratch_shapes=[pltpu.VMEM((dk, dv), F32)],
        compiler_params=pltpu.CompilerParams(
            dimension_semantics=("arbitrary", "arbitrary", "arbitrary"),
            vmem_limit_bytes=VMEM_LIMIT),
        name="retention_mixer",
    )(proj3d, proj3d, proj3d, proj3d, cos, sin, decay_mat, q_dec, k_dec,
      chunk_dec, gn_g)


LOG2E = 1.4426950408889634


SB_EXIT_MASS = 152.0
SB_Z_CAP = 64.0


def _sb_tile_terms(q, k_tile, tri, mask):
    z = lax.dot_general(q, k_tile, (((1,), (1,)), ((), ())),
                        preferred_element_type=F32)
    sp = jnp.maximum(z, jnp.log2(1.0 + jnp.exp2(jnp.minimum(z, SB_Z_CAP))))
    if mask is not None:
        sp = jnp.where(mask, sp, 0.0)
    after = jnp.dot(sp.astype(BF16), tri, preferred_element_type=F32)
    return (z - sp) - after, after[:, 0:1] + sp[:, 0:1]


def _sb_weights(base, r_prev, mask):
    a = jnp.exp2(base - r_prev)
    if mask is not None:
        a = jnp.where(mask, a, 0.0)
    return a.astype(BF16)


def _sb_kernel(q_ref, k_ref, v_ref, g_ref, o_ref, acc_ref, r_ref):
    step = pl.program_id(2)
    lane = lax.broadcasted_iota(jnp.int32, (SB_TQ, LANES), 1)
    row = lax.broadcasted_iota(jnp.int32, (SB_TK, SB_TK), 0)
    col = lax.broadcasted_iota(jnp.int32, (SB_TK, SB_TK), 1)
    tri = jnp.where(row > col, 1.0, 0.0).astype(BF16)
    diag_mask = col < row

    def q_heads(sub):
        qf = q_ref[0, sub * SB_TQ:(sub + 1) * SB_TQ, :].astype(F32)
        qf = qf * (SB_QK ** -0.5 * LOG2E)
        return [jnp.where(lane < SB_QK, qf, 0.0).astype(BF16),
                jnp.where(lane >= SB_QK, qf, 0.0).astype(BF16)]

    def add_tiles(sub, kb_near, n, near_mask, first):
        q = q_heads(sub)
        kbs = [kb_near - t for t in range(n)]
        starts = [pl.multiple_of(jnp.maximum(kb, 0) * SB_TK, SB_TK) for kb in kbs]
        k_tiles = [k_ref[0, pl.ds(s, SB_TK), :] for s in starts]
        v_tiles = [v_ref[0, pl.ds(s, SB_TK), :] for s in starts]
        r_min = []
        for h in range(2):
            vcols = slice(h * SB_V, (h + 1) * SB_V)
            terms = [_sb_tile_terms(q[h], k_tiles[t], tri,
                                    near_mask if t == 0 else None)
                     for t in range(n)]
            r = jnp.zeros((SB_TQ, 1), F32) if first else r_ref[sub, h]
            contrib = None
            for t in range(n):
                base, row_sum = terms[t]
                a = _sb_weights(base, r, near_mask if t == 0 else None)
                c = jnp.dot(a, v_tiles[t][:, vcols], preferred_element_type=F32)
                if t == 0:
                    contrib = c
                    r = r + row_sum
                else:
                    contrib = contrib + jnp.where(kbs[t] >= 0, c, 0.0)
                    r = r + jnp.where(kbs[t] >= 0, row_sum, 0.0)
            if first:
                acc_ref[sub, h] = contrib
            else:
                acc_ref[sub, h] += contrib
            r_ref[sub, h] = r
            r_min.append(jnp.min(r))
        return jnp.minimum(r_min[0], r_min[1])

    blocks = [step * SB_SUBS + sub for sub in range(SB_SUBS)]
    masses = tuple(add_tiles(sub, blocks[sub], SB_FIRST_TILES, diag_mask, True)
                   for sub in range(SB_SUBS))
    n_more = [(qb + 1 - SB_FIRST_TILES + SB_LOOP_TILES - 1) // SB_LOOP_TILES
              for qb in blocks]

    def unfinished(j, mass, sub):
        return jnp.logical_and(j < n_more[sub], mass < SB_EXIT_MASS)

    def more_tiles(carry):
        j, masses = carry
        todo = [unfinished(j, masses[sub], sub) for sub in range(SB_SUBS)]
        return functools.reduce(jnp.logical_or, todo)

    def body(carry):
        j, masses = carry
        new = []
        for sub in range(SB_SUBS):
            kb = blocks[sub] - SB_FIRST_TILES - SB_LOOP_TILES * j
            new.append(lax.cond(
                unfinished(j, masses[sub], sub),
                functools.partial(add_tiles, sub, kb, SB_LOOP_TILES, None, False),
                functools.partial(lambda m: m, masses[sub])))
        return j + 1, tuple(new)

    lax.while_loop(more_tiles, body, (jnp.int32(0), masses))

    for sub in range(SB_SUBS):
        rows = slice(sub * SB_TQ, (sub + 1) * SB_TQ)
        for h in range(2):
            cols = slice(h * SB_V, (h + 1) * SB_V)
            gate = _silu(g_ref[0, rows, cols].astype(F32))
            o_ref[0, rows, cols] = (acc_ref[sub, h] * gate).astype(BF16)


def _stick_breaking(proj3d):
    B, S, _ = proj3d.shape
    pair_v = 2 * SB_V
    step_rows = SB_SUBS * SB_TQ
    k_off = D_MODEL // LANES
    v_off = 2 * D_MODEL // pair_v
    g_off = (2 * D_MODEL + D_INNER) // pair_v
    return pl.pallas_call(
        _sb_kernel,
        grid=(B, SB_HEADS // 2, S // step_rows),
        in_specs=[
            pl.BlockSpec((1, step_rows, LANES), lambda b, p, i: (b, i, p)),
            pl.BlockSpec((1, S, LANES), lambda b, p, i: (b, 0, k_off + p)),
            pl.BlockSpec((1, S, pair_v), lambda b, p, i: (b, 0, v_off + p)),
            pl.BlockSpec((1, step_rows, pair_v), lambda b, p, i: (b, i, g_off + p)),
        ],
        out_specs=pl.BlockSpec((1, step_rows, pair_v), lambda b, p, i: (b, i, p)),
        out_shape=jax.ShapeDtypeStruct((B, S, D_INNER), BF16),
        scratch_shapes=[pltpu.VMEM((SB_SUBS, 2, SB_TQ, SB_V), F32),
                        pltpu.VMEM((SB_SUBS, 2, SB_TQ, 1), F32)],
        compiler_params=pltpu.CompilerParams(
            dimension_semantics=("arbitrary", "arbitrary", "arbitrary"),
            vmem_limit_bytes=VMEM_LIMIT),
        name="stick_breaking_mixer",
    )(proj3d, proj3d, proj3d, proj3d)


def kernel(x, c, w_in, w_out, w_mod, b_mod, pre_norm, post_norm, ret_gn):
    B, S, D = x.shape
    assert D == D_MODEL and S % ROW_TILE == 0 and S % RET_ROWS == 0
    assert S % OUT_ROW_TILE == 0
    assert S % (SB_SUBS * SB_TQ) == 0 and SB_TQ == SB_TK
    mod = _modulation(c, w_mod, b_mod)
    w_in_b = w_in.astype(BF16)
    w_out_b = w_out.astype(BF16)
    tables = _retention_tables(S)
    x2d = x.reshape(B * S, D)
    for i in range(DEPTH):
        shift = mod[i, :, :D].reshape(B, 1, D)
        scale = mod[i, :, D:2 * D].reshape(B, 1, D)
        gate = mod[i, :, 2 * D:].reshape(B, 1, D)
        proj = _in_projection(x2d, pre_norm[i].reshape(1, D), shift, scale,
                              w_in_b[i], S)
        proj3d = proj.reshape(B, S, D_PROJ)
        if i % 2 == 0:
            yg = _retention(proj3d, ret_gn[i // 2].reshape(1, D_INNER), tables)
        else:
            yg = _stick_breaking(proj3d)
        x2d = _out_projection(yg.reshape(B * S, D_INNER), w_out_b[i], x2d,
                              post_norm[i].reshape(1, D), gate, S, in_place=i > 0)
    return x2d.reshape(B, S, D)
```

```python
import functools

import jax
import jax.numpy as jnp
from jax import lax
from jax.experimental import pallas as pl
from jax.experimental.pallas import tpu as pltpu

D_MODEL = 1024
DEPTH = 4
D_INNER = 2 * D_MODEL
RET_HEADS = 4
RET_QK = D_MODEL // RET_HEADS
RET_V = D_INNER // RET_HEADS
RET_CHUNK = 256
SB_HEADS = 16
SB_QK = D_MODEL // SB_HEADS
SB_V = D_INNER // SB_HEADS
D_PROJ = 2 * D_MODEL + 2 * D_INNER
RMS_EPS = 1e-6
GN_EPS = 1e-5
ROPE_BASE = 10000.0

LANES = 128
VMEM_LIMIT = 56 * 1024 * 1024

ROW_TILE = 1024
OUT_ROW_TILE = 1024
N_TILE = 512
RET_ROWS = 2048
SB_TQ = 256
SB_TK = 256
SB_SUBS = 4
SB_FIRST_TILES = 3
SB_LOOP_TILES = 2

BF16 = jnp.bfloat16
F32 = jnp.float32


def _silu(v):
    return v / (1.0 + jnp.exp(-v))


def _split_bf16(v):
    hi = v.astype(BF16)
    lo = (v - hi.astype(F32)).astype(BF16)
    return hi, lo


def _mod_kernel(c_ref, w_ref, b_ref, o_ref):
    a_hi, a_lo = _split_bf16(_silu(c_ref[...]))
    w_hi, w_lo = _split_bf16(w_ref[0])
    acc = jnp.dot(a_hi, w_hi, preferred_element_type=F32)
    acc += jnp.dot(a_hi, w_lo, preferred_element_type=F32)
    acc += jnp.dot(a_lo, w_hi, preferred_element_type=F32)
    o_ref[0] = acc + b_ref[0]


def _modulation(c, w_mod, b_mod):
    B = c.shape[0]
    return pl.pallas_call(
        _mod_kernel,
        grid=(DEPTH, 3),
        in_specs=[
            pl.BlockSpec((B, D_MODEL), lambda i, j: (0, 0)),
            pl.BlockSpec((1, D_MODEL, D_MODEL), lambda i, j: (i, 0, j)),
            pl.BlockSpec((1, 1, D_MODEL), lambda i, j: (i, 0, j)),
        ],
        out_specs=pl.BlockSpec((1, B, D_MODEL), lambda i, j: (i, 0, j)),
        out_shape=jax.ShapeDtypeStruct((DEPTH, B, 3 * D_MODEL), F32),
        compiler_params=pltpu.CompilerParams(
            dimension_semantics=("arbitrary", "arbitrary"),
            vmem_limit_bytes=VMEM_LIMIT),
        name="adaln_modulation",
    )(c, w_mod, b_mod.reshape(DEPTH, 1, 3 * D_MODEL))


def _inproj_kernel(x_ref, g_ref, shift_ref, scale_ref, w_ref, o_ref):
    xf = x_ref[...]
    ms = jnp.mean(xf * xf, axis=-1, keepdims=True)
    y = xf * lax.rsqrt(ms + RMS_EPS) * g_ref[...]
    h = (y * (1.0 + scale_ref[0]) + shift_ref[0]).astype(BF16)
    for n in range(D_PROJ // N_TILE):
        cols = slice(n * N_TILE, (n + 1) * N_TILE)
        o_ref[:, cols] = jnp.dot(
            h, w_ref[:, cols], preferred_element_type=F32).astype(BF16)


def _in_projection(x2d, pre_g, shift, scale, w_bf16, seq):
    M = x2d.shape[0]
    steps_per_batch = seq // ROW_TILE
    return pl.pallas_call(
        _inproj_kernel,
        grid=(M // ROW_TILE,),
        in_specs=[
            pl.BlockSpec((ROW_TILE, D_MODEL), lambda i: (i, 0)),
            pl.BlockSpec((1, D_MODEL), lambda i: (0, 0)),
            pl.BlockSpec((1, 1, D_MODEL), lambda i: (i // steps_per_batch, 0, 0)),
            pl.BlockSpec((1, 1, D_MODEL), lambda i: (i // steps_per_batch, 0, 0)),
            pl.BlockSpec((D_MODEL, D_PROJ), lambda i: (0, 0),
                         pipeline_mode=pl.Buffered(1)),
        ],
        out_specs=pl.BlockSpec((ROW_TILE, D_PROJ), lambda i: (i, 0)),
        out_shape=jax.ShapeDtypeStruct((M, D_PROJ), BF16),
        compiler_params=pltpu.CompilerParams(
            dimension_semantics=("arbitrary",),
            vmem_limit_bytes=VMEM_LIMIT),
        name="norm_inproj",
    )(x2d, pre_g, shift, scale, w_bf16)


def _outproj_kernel(yg_ref, w_ref, x_ref, g_ref, gate_ref, o_ref):
    y = jnp.dot(yg_ref[...], w_ref[...], preferred_element_type=F32)
    ms = jnp.mean(y * y, axis=-1, keepdims=True)
    yn = y * lax.rsqrt(ms + RMS_EPS) * g_ref[...]
    o_ref[...] = x_ref[...] + gate_ref[0] * yn


def _out_projection(yg2d, w_bf16, x2d, post_g, gate, seq, in_place):
    M = x2d.shape[0]
    steps_per_batch = seq // OUT_ROW_TILE
    return pl.pallas_call(
        _outproj_kernel,
        grid=(M // OUT_ROW_TILE,),
        in_specs=[
            pl.BlockSpec((OUT_ROW_TILE, D_INNER), lambda i: (i, 0)),
            pl.BlockSpec((D_INNER, D_MODEL), lambda i: (0, 0),
                         pipeline_mode=pl.Buffered(1)),
            pl.BlockSpec((OUT_ROW_TILE, D_MODEL), lambda i: (i, 0)),
            pl.BlockSpec((1, D_MODEL), lambda i: (0, 0)),
            pl.BlockSpec((1, 1, D_MODEL), lambda i: (i // steps_per_batch, 0, 0)),
        ],
        out_specs=pl.BlockSpec((OUT_ROW_TILE, D_MODEL), lambda i: (i, 0)),
        out_shape=jax.ShapeDtypeStruct((M, D_MODEL), F32),
        input_output_aliases={2: 0} if in_place else {},
        compiler_params=pltpu.CompilerParams(
            dimension_semantics=("arbitrary",),
            vmem_limit_bytes=VMEM_LIMIT),
        name="outproj_residual",
    )(yg2d, w_bf16, x2d, post_g, gate)


def _retention_tables(seq):
    H, dk, C = RET_HEADS, RET_QK, RET_CHUNK
    omega = 1.0 / (ROPE_BASE ** jnp.linspace(0.0, 1.0, dk // 2, dtype=F32))
    ang = jnp.arange(seq, dtype=F32)[:, None] * omega[None, :]
    cos, sin = jnp.cos(ang), jnp.sin(ang)
    log_gamma = jnp.log1p(-jnp.exp2(jnp.linspace(-5.0, -9.0, H, dtype=F32)))
    idx = jnp.arange(C, dtype=F32)
    diff = idx[:, None] - idx[None, :]
    decay_mat = jnp.where(
        diff[None] >= 0,
        jnp.exp(jnp.maximum(diff, 0.0)[None] * log_gamma[:, None, None]), 0.0)
    q_dec = jnp.exp((idx[None, :] + 1.0) * log_gamma[:, None])
    k_dec = jnp.exp((C - 1.0 - idx[None, :]) * log_gamma[:, None])
    chunk_dec = jnp.exp(C * log_gamma)
    decay_mat = decay_mat * dk ** -0.5
    k_dec = k_dec * dk ** -0.5
    q_dec = jnp.broadcast_to(q_dec[:, :, None], (H, C, dk))
    k_dec = jnp.broadcast_to(k_dec[:, :, None], (H, C, dk))
    chunk_dec = jnp.broadcast_to(chunk_dec[:, None, None], (H, 1, RET_V))
    return cos, sin, decay_mat, q_dec, k_dec, chunk_dec


def _rotate(t, cos, sin):
    half = RET_QK // 2
    t1, t2 = t[:, :half], t[:, half:]
    return jnp.concatenate([t1 * cos - t2 * sin, t2 * cos + t1 * sin], axis=-1)


def _retention_kernel(q_ref, k_ref, v_ref, g_ref, cos_ref, sin_ref, dm_ref,
                      qd_ref, kd_ref, cd_ref, gn_ref, o_ref, state_ref):
    @pl.when(pl.program_id(2) == 0)
    def _():
        state_ref[...] = jnp.zeros_like(state_ref)

    C = RET_CHUNK
    for c in range(RET_ROWS // C):
        rows = slice(c * C, (c + 1) * C)
        cos, sin = cos_ref[rows, :], sin_ref[rows, :]
        q = _rotate(q_ref[0, rows, :].astype(F32), cos, sin)
        k = _rotate(k_ref[0, rows, :].astype(F32), cos, sin)
        v = v_ref[0, rows, :]
        state = state_ref[...]
        scores = lax.dot_general(
            q.astype(BF16), k.astype(BF16), (((1,), (1,)), ((), ())),
            preferred_element_type=F32) * dm_ref[0]
        inner = jnp.dot(scores.astype(BF16), v, preferred_element_type=F32)
        cross = jnp.dot((q * qd_ref[0]).astype(BF16), state.astype(BF16),
                        preferred_element_type=F32)
        kv = lax.dot_general(
            (k * kd_ref[0]).astype(BF16), v, (((0,), (0,)), ((), ())),
            preferred_element_type=F32)
        state_ref[...] = state * cd_ref[0] + kv
        o = inner + cross
        mu = jnp.mean(o, axis=-1, keepdims=True)
        d = o - mu
        var = jnp.mean(d * d, axis=-1, keepdims=True)
        on = d * lax.rsqrt(var + GN_EPS) * gn_ref[...]
        o_ref[0, rows, :] = (on * _silu(g_ref[0, rows, :].astype(F32))).astype(BF16)


def _retention(proj3d, gn_g, tables):
    B, S, _ = proj3d.shape
    cos, sin, decay_mat, q_dec, k_dec, chunk_dec = tables
    H, dk, dv, C = RET_HEADS, RET_QK, RET_V, RET_CHUNK
    k_off = D_MODEL // dk
    v_off = 2 * D_MODEL // dv
    g_off = (2 * D_MODEL + D_INNER) // dv
    return pl.pallas_call(
        _retention_kernel,
        grid=(B, H, S // RET_ROWS),
        in_specs=[
            pl.BlockSpec((1, RET_ROWS, dk), lambda b, h, s: (b, s, h)),
            pl.BlockSpec((1, RET_ROWS, dk), lambda b, h, s: (b, s, k_off + h)),
            pl.BlockSpec((1, RET_ROWS, dv), lambda b, h, s: (b, s, v_off + h)),
            pl.BlockSpec((1, RET_ROWS, dv), lambda b, h, s: (b, s, g_off + h)),
            pl.BlockSpec((RET_ROWS, dk // 2), lambda b, h, s: (s, 0)),
            pl.BlockSpec((RET_ROWS, dk // 2), lambda b, h, s: (s, 0)),
            pl.BlockSpec((1, C, C), lambda b, h, s: (h, 0, 0)),
            pl.BlockSpec((1, C, dk), lambda b, h, s: (h, 0, 0)),
            pl.BlockSpec((1, C, dk), lambda b, h, s: (h, 0, 0)),
            pl.BlockSpec((1, 1, dv), lambda b, h, s: (h, 0, 0)),
            pl.BlockSpec((1, dv), lambda b, h, s: (0, h)),
        ],
        out_specs=pl.BlockSpec((1, RET_ROWS, dv), lambda b, h, s: (b, s, h)),
        out_shape=jax.ShapeDtypeStruct((B, S, D_INNER), BF16),
        scratch_shapes=[pltpu.VMEM((dk, dv), F32)],
        compiler_params=pltpu.CompilerParams(
            dimension_semantics=("arbitrary", "arbitrary", "arbitrary"),
            vmem_limit_bytes=VMEM_LIMIT),
        name="retention_mixer",
    )(proj3d, proj3d, proj3d, proj3d, cos, sin, decay_mat, q_dec, k_dec,
      chunk_dec, gn_g)


LOG2E = 1.4426950408889634


SB_EXIT_MASS = 152.0
SB_Z_CAP = 64.0


def _sb_tile_terms(q, k_tile, tri, mask):
    z = lax.dot_general(q, k_tile, (((1,), (1,)), ((), ())),
                        preferred_element_type=F32)
    sp = jnp.maximum(z, jnp.log2(1.0 + jnp.exp2(jnp.minimum(z, SB_Z_CAP))))
    if mask is not None:
        sp = jnp.where(mask, sp, 0.0)
    after = jnp.dot(sp.astype(BF16), tri, preferred_element_type=F32)
    return (z - sp) - after, after[:, 0:1] + sp[:, 0:1]


def _sb_weights(base, r_prev, mask):
    a = jnp.exp2(base - r_prev)
    if mask is not None:
        a = jnp.where(mask, a, 0.0)
    return a.astype(BF16)


def _sb_kernel(q_ref, k_ref, v_ref, g_ref, o_ref, acc_ref, r_ref):
    step = pl.program_id(2)
    lane = lax.broadcasted_iota(jnp.int32, (SB_TQ, LANES), 1)
    row = lax.broadcasted_iota(jnp.int32, (SB_TK, SB_TK), 0)
    col = lax.broadcasted_iota(jnp.int32, (SB_TK, SB_TK), 1)
    tri = jnp.where(row > col, 1.0, 0.0).astype(BF16)
    diag_mask = col < row

    def q_heads(sub):
        qf = q_ref[0, sub * SB_TQ:(sub + 1) * SB_TQ, :].astype(F32)
        qf = qf * (SB_QK ** -0.5 * LOG2E)
        return [jnp.where(lane < SB_QK, qf, 0.0).astype(BF16),
                jnp.where(lane >= SB_QK, qf, 0.0).astype(BF16)]

    def add_tiles(sub, kb_near, n, near_mask, first):
        q = q_heads(sub)
        kbs = [kb_near - t for t in range(n)]
        starts = [pl.multiple_of(jnp.maximum(kb, 0) * SB_TK, SB_TK) for kb in kbs]
        k_tiles = [k_ref[0, pl.ds(s, SB_TK), :] for s in starts]
        v_tiles = [v_ref[0, pl.ds(s, SB_TK), :] for s in starts]
        r_min = []
        for h in range(2):
            vcols = slice(h * SB_V, (h + 1) * SB_V)
            terms = [_sb_tile_terms(q[h], k_tiles[t], tri,
                                    near_mask if t == 0 else None)
                     for t in range(n)]
            r = jnp.zeros((SB_TQ, 1), F32) if first else r_ref[sub, h]
            contrib = None
            for t in range(n):
                base, row_sum = terms[t]
                a = _sb_weights(base, r, near_mask if t == 0 else None)
                c = jnp.dot(a, v_tiles[t][:, vcols], preferred_element_type=F32)
                if t == 0:
                    contrib = c
                    r = r + row_sum
                else:
                    contrib = contrib + jnp.where(kbs[t] >= 0, c, 0.0)
                    r = r + jnp.where(kbs[t] >= 0, row_sum, 0.0)
            if first:
                acc_ref[sub, h] = contrib
            else:
                acc_ref[sub, h] += contrib
            r_ref[sub, h] = r
            r_min.append(jnp.min(r))
        return jnp.minimum(r_min[0], r_min[1])

    blocks = [step * SB_SUBS + sub for sub in range(SB_SUBS)]
    masses = tuple(add_tiles(sub, blocks[sub], SB_FIRST_TILES, diag_mask, True)
                   for sub in range(SB_SUBS))
    n_more = [(qb + 1 - SB_FIRST_TILES + SB_LOOP_TILES - 1) // SB_LOOP_TILES
              for qb in blocks]

    def unfinished(j, mass, sub):
        return jnp.logical_and(j < n_more[sub], mass < SB_EXIT_MASS)

    def more_tiles(carry):
        j, masses = carry
        todo = [unfinished(j, masses[sub], sub) for sub in range(SB_SUBS)]
        return functools.reduce(jnp.logical_or, todo)

    def body(carry):
        j, masses = carry
        new = []
        for sub in range(SB_SUBS):
            kb = blocks[sub] - SB_FIRST_TILES - SB_LOOP_TILES * j
            new.append(lax.cond(
                unfinished(j, masses[sub], sub),
                functools.partial(add_tiles, sub, kb, SB_LOOP_TILES, None, False),
                functools.partial(lambda m: m, masses[sub])))
        return j + 1, tuple(new)

    lax.while_loop(more_tiles, body, (jnp.int32(0), masses))

    for sub in range(SB_SUBS):
        rows = slice(sub * SB_TQ, (sub + 1) * SB_TQ)
        for h in range(2):
            cols = slice(h * SB_V, (h + 1) * SB_V)
            gate = _silu(g_ref[0, rows, cols].astype(F32))
            o_ref[0, rows, cols] = (acc_ref[sub, h] * gate).astype(BF16)


def _stick_breaking(proj3d):
    B, S, _ = proj3d.shape
    pair_v = 2 * SB_V
    step_rows = SB_SUBS * SB_TQ
    k_off = D_MODEL // LANES
    v_off = 2 * D_MODEL // pair_v
    g_off = (2 * D_MODEL + D_INNER) // pair_v
    return pl.pallas_call(
        _sb_kernel,
        grid=(B, SB_HEADS // 2, S // step_rows),
        in_specs=[
            pl.BlockSpec((1, step_rows, LANES), lambda b, p, i: (b, i, p)),
            pl.BlockSpec((1, S, LANES), lambda b, p, i: (b, 0, k_off + p)),
            pl.BlockSpec((1, S, pair_v), lambda b, p, i: (b, 0, v_off + p)),
            pl.BlockSpec((1, step_rows, pair_v), lambda b, p, i: (b, i, g_off + p)),
        ],
        out_specs=pl.BlockSpec((1, step_rows, pair_v), lambda b, p, i: (b, i, p)),
        out_shape=jax.ShapeDtypeStruct((B, S, D_INNER), BF16),
        scratch_shapes=[pltpu.VMEM((SB_SUBS, 2, SB_TQ, SB_V), F32),
                        pltpu.VMEM((SB_SUBS, 2, SB_TQ, 1), F32)],
        compiler_params=pltpu.CompilerParams(
            dimension_semantics=("arbitrary", "arbitrary", "arbitrary"),
            vmem_limit_bytes=VMEM_LIMIT),
        name="stick_breaking_mixer",
    )(proj3d, proj3d, proj3d, proj3d)


def kernel(x, c, w_in, w_out, w_mod, b_mod, pre_norm, post_norm, ret_gn):
    B, S, D = x.shape
    assert D == D_MODEL and S % ROW_TILE == 0 and S % RET_ROWS == 0
    assert S % OUT_ROW_TILE == 0
    assert S % (SB_SUBS * SB_TQ) == 0 and SB_TQ == SB_TK
    mod = _modulation(c, w_mod, b_mod)
    w_in_b = w_in.astype(BF16)
    w_out_b = w_out.astype(BF16)
    tables = _retention_tables(S)
    x2d = x.reshape(B * S, D)
    for i in range(DEPTH):
        shift = mod[i, :, :D].reshape(B, 1, D)
        scale = mod[i, :, D:2 * D].reshape(B, 1, D)
        gate = mod[i, :, 2 * D:].reshape(B, 1, D)
        proj = _in_projection(x2d, pre_norm[i].reshape(1, D), shift, scale,
                              w_in_b[i], S)
        proj3d = proj.reshape(B, S, D_PROJ)
        if i % 2 == 0:
            yg = _retention(proj3d, ret_gn[i // 2].reshape(1, D_INNER), tables)
        else:
            yg = _stick_breaking(proj3d)
        x2d = _out_projection(yg.reshape(B * S, D_INNER), w_out_b[i], x2d,
                              post_norm[i].reshape(1, D), gate, S, in_place=i > 0)
    return x2d.reshape(B, S, D)
```
